```python
import jax
import jax.numpy as jnp
from jax import lax
import numpy as np

D_MODEL = 4096
BATCH = 8
SEQ = 2048
DEPTH = 2

CTX_LEN = 256
GRID_W = 64
EPS = 1e-6
ROPE_THETA = 10000.0

MLA_HEADS = 16
MLA_Q_RANK = 1024
MLA_KV_RANK = 512
MLA_NOPE = 128
MLA_ROPE = 64
MLA_V = 128
MLA_QBLOCK = 128
GLA_HEADS = 8
GLA_DK = 128
GLA_DV = 256
GLA_GATE_RANK = 16
GLA_TAU = 16.0
HGRN_HEADS = 16
HGRN_DK = 128
HGRN_DV = 128
SCAN_CHUNK = 64
N_BRANCHES = 3
BRANCH_WIDTH = 2048
MOE_GROUPS = 4
MOE_EXPERTS_PER_GROUP = 4
MOE_EXPERTS = 16
MOE_TOP_K = 2
MOE_HIDDEN = 768
MOE_BLOCK = 256

IN_WIDTHS = (
    MLA_Q_RANK, MLA_KV_RANK, MLA_ROPE,
    GLA_HEADS * GLA_DK, GLA_HEADS * GLA_DK, GLA_HEADS * GLA_DV,
    GLA_HEADS * GLA_DV, GLA_GATE_RANK, GLA_GATE_RANK,
    HGRN_HEADS * HGRN_DK, HGRN_HEADS * HGRN_DV,
    HGRN_HEADS * HGRN_DK, HGRN_HEADS * HGRN_DK, HGRN_HEADS * HGRN_DV,
    D_MODEL, D_MODEL, D_MODEL,
)
IN_COLS = sum(IN_WIDTHS)

kernel_name = 'hybrid_mla_gla_hgrn2_hmoe_block'


def rmsnorm(x, g):
    xf = x.astype(jnp.float32)
    y = xf * lax.rsqrt(jnp.mean(xf * xf, axis=-1, keepdims=True) + EPS)
    return (y * g.astype(jnp.float32)).astype(x.dtype)


def modulate(v, g, shift, scale):
    return rmsnorm(v, g) * (1.0 + scale) + shift


def split_cols(a, widths):
    return jnp.split(a, np.cumsum(widths)[:-1].tolist(), axis=-1)


def to_heads(a, n_heads):
    b, t, _ = a.shape
    return a.reshape(b, t, n_heads, -1).transpose(0, 2, 1, 3)


def from_heads(a):
    b, h, t, d = a.shape
    return a.transpose(0, 2, 1, 3).reshape(b, t, h * d)


def axial_rope_tables(n_rows):
    quarter = MLA_ROPE // 4
    inv_freq = ROPE_THETA ** (-jnp.arange(quarter, dtype=jnp.float32) / quarter)
    row = jnp.repeat(jnp.arange(n_rows, dtype=jnp.float32), GRID_W)
    col = jnp.tile(jnp.arange(GRID_W, dtype=jnp.float32), n_rows)
    ang = jnp.concatenate([row[:, None] * inv_freq, col[:, None] * inv_freq], axis=-1)
    return jnp.cos(ang), jnp.sin(ang)


def apply_axial_rope(u, cos, sin):
    q = MLA_ROPE // 4
    uf = u.astype(jnp.float32)
    r1, r2, c1, c2 = uf[..., :q], uf[..., q:2 * q], uf[..., 2 * q:3 * q], uf[..., 3 * q:]
    cr, cc, sr, sc = cos[:, :q], cos[:, q:], sin[:, :q], sin[:, q:]
    out = jnp.concatenate([r1 * cr - r2 * sr, r1 * sr + r2 * cr,
                           c1 * cc - c2 * sc, c1 * sc + c2 * cc], axis=-1)
    return out.astype(u.dtype)


def softmax_attention(q, k, v):
    s = jnp.einsum('bhqd,bhkd->bhqk', q, k, preferred_element_type=jnp.float32) * (q.shape[-1] ** -0.5)
    p = jax.nn.softmax(s, axis=-1)
    return jnp.einsum('bhqk,bhkd->bhqd', p.astype(v.dtype), v)


def blocked_attention(q, k, v):
    b, h, t, d = q.shape
    nb = t // MLA_QBLOCK
    qb = jnp.moveaxis(q.reshape(b, h, nb, MLA_QBLOCK, d), 2, 0)
    ob = lax.map(lambda qi: softmax_attention(qi, k, v), qb)
    return jnp.moveaxis(ob, 0, 2).reshape(b, h, t, v.shape[-1])


def chunk_gated_scan(q, k, v, log_g, s0):
    b, h, t, _ = q.shape
    dv = v.shape[-1]
    n = t // SCAN_CHUNK

    def chunks(a):
        return jnp.moveaxis(a.astype(jnp.float32).reshape(b, h, n, SCAN_CHUNK, a.shape[-1]), 2, 0)

    lower = jnp.tril(jnp.ones((SCAN_CHUNK, SCAN_CHUNK), dtype=bool))[:, :, None]

    def step(state, inp):
        qi, ki, vi, gi = inp
        cum = jnp.cumsum(gi, axis=2)
        o_inter = jnp.einsum('bhtk,bhkv->bhtv', qi * jnp.exp(cum), state)
        rel = jnp.exp(jnp.where(lower, cum[:, :, :, None, :] - cum[:, :, None, :, :], -jnp.inf))
        scores = jnp.einsum('bhtk,bhsk,bhtsk->bhts', qi, ki, rel)
        o = o_inter + jnp.einsum('bhts,bhsv->bhtv', scores, vi)
        last = cum[:, :, -1:, :]
        new_state = state * jnp.exp(last[:, :, 0, :, None]) + jnp.einsum(
            'bhsk,bhsv->bhkv', ki * jnp.exp(last - cum), vi)
        return new_state, o

    state, o = lax.scan(step, s0, (chunks(q), chunks(k), chunks(v), chunks(log_g)))
    return jnp.moveaxis(o, 0, 2).reshape(b, h, t, dv), state


def bidirectional_scan(q, k_fwd, k_bwd, v, lg_fwd, lg_bwd, n_ctx):
    b, h, _, dk = q.shape
    s0 = jnp.zeros((b, h, dk, v.shape[-1]), jnp.float32)
    cp = lambda a: a[:, :, :n_ctx]
    lp = lambda a: a[:, :, n_ctx:]
    rev = lambda a: jnp.flip(a, axis=2)
    o_cf, s_f = chunk_gated_scan(cp(q), cp(k_fwd), cp(v), cp(lg_fwd), s0)
    o_lf, _ = chunk_gated_scan(lp(q), lp(k_fwd), lp(v), lp(lg_fwd), s_f)
    o_cb, s_b = chunk_gated_scan(rev(cp(q)), rev(cp(k_bwd)), rev(cp(v)), rev(cp(lg_bwd)), s0)
    o_lb, _ = chunk_gated_scan(rev(lp(q)), rev(lp(k_bwd)), rev(lp(v)), rev(lp(lg_bwd)), s_b)
    return o_cf + rev(o_cb), o_lf + rev(o_lb)


def hybrid_mixer(h, n_ctx, rope_cos, rope_sin, with_ctx, w_in, mla_q_norm_g, mla_kv_norm_g,
                 mla_w_uq, mla_w_ukv, gla_w_a2, gla_b_a, gla_onorm_g, hgrn_lb, hgrn_onorm_g,
                 w_branch, w_out):
    f32 = jnp.float32
    bsz, t_all, _ = h.shape
    proj = h @ w_in
    (cq, ckv, kr, gq, gk, gv, gr, ga_f, ga_b,
     hq, hi, hf_f, hf_b, hg, z_a, z_b, z_c) = split_cols(proj, IN_WIDTHS)
    cp = lambda a: a[:, :n_ctx]
    lp = lambda a: a[:, n_ctx:]

    q = jnp.einsum('btr,rhd->bhtd', rmsnorm(cq, mla_q_norm_g), mla_w_uq)
    kv = jnp.einsum('btr,rhd->bhtd', rmsnorm(ckv, mla_kv_norm_g), mla_w_ukv)
    k_nope, v_a = kv[..., :MLA_NOPE], kv[..., MLA_NOPE:]
    q_lat = q[:, :, n_ctx:]
    q_lat = jnp.concatenate([q_lat[..., :MLA_NOPE],
                             apply_axial_rope(q_lat[..., MLA_NOPE:], rope_cos, rope_sin)], axis=-1)
    k_rope = jnp.concatenate([cp(kr), apply_axial_rope(lp(kr), rope_cos, rope_sin)], axis=1)
    k_a = jnp.concatenate([k_nope, jnp.broadcast_to(k_rope[:, None], (bsz, MLA_HEADS, t_all, MLA_ROPE))], axis=-1)
    o_a_lat = from_heads(blocked_attention(q_lat, k_a, v_a)).astype(h.dtype)

    def gla_log_gate(a, d):
        logit = jnp.einsum('btr,rk->btk', a, gla_w_a2[d]).astype(f32) + gla_b_a[d].astype(f32)
        return to_heads(jax.nn.log_sigmoid(logit) / GLA_TAU, GLA_HEADS)
    q_b = to_heads(gq, GLA_HEADS) * (GLA_DK ** -0.5)
    k_b = to_heads(gk, GLA_HEADS)
    o_b_ctx, o_b_lat = bidirectional_scan(q_b, k_b, k_b, to_heads(gv, GLA_HEADS),
                                          gla_log_gate(ga_f, 0), gla_log_gate(ga_b, 1), n_ctx)

    lb = hgrn_lb.astype(f32)
    def hgrn_gates(f_logit):
        fl = f_logit.astype(f32)
        log_f = jnp.logaddexp(jnp.log(lb), jnp.log1p(-lb) + jax.nn.log_sigmoid(fl))
        key_in = (1.0 - lb) * jax.nn.sigmoid(-fl)
        return to_heads(key_in, HGRN_HEADS), to_heads(log_f, HGRN_HEADS)
    k_cf, lg_cf = hgrn_gates(hf_f)
    k_cb, lg_cb = hgrn_gates(hf_b)
    o_c_ctx, o_c_lat = bidirectional_scan(to_heads(hq, HGRN_HEADS), k_cf, k_cb, to_heads(hi, HGRN_HEADS),
                                          lg_cf, lg_cb, n_ctx)

    def gla_out(o, r):
        return (jax.nn.silu(r.astype(f32)) * from_heads(rmsnorm(o, gla_onorm_g))).astype(h.dtype)

    def hgrn_out(o, g):
        gated = o * jax.nn.sigmoid(to_heads(g, HGRN_HEADS).astype(f32))
        return from_heads(rmsnorm(gated, hgrn_onorm_g)).astype(h.dtype)

    def merge(o_a, o_b, o_c, za, zb, zc):
        m = (jax.nn.sigmoid(za) * (o_a @ w_branch[0])
             + jax.nn.sigmoid(zb) * (o_b @ w_branch[1])
             + jax.nn.sigmoid(zc) * (o_c @ w_branch[2]))
        return m @ w_out

    y_lat = merge(o_a_lat, gla_out(o_b_lat, lp(gr)), hgrn_out(o_c_lat, lp(hg)), lp(z_a), lp(z_b), lp(z_c))
    if not with_ctx:
        return None, y_lat
    o_a_ctx = from_heads(softmax_attention(q[:, :, :n_ctx], k_a[:, :, :n_ctx], v_a[:, :, :n_ctx])).astype(h.dtype)
    y_ctx = merge(o_a_ctx, gla_out(o_b_ctx, cp(gr)), hgrn_out(o_c_ctx, cp(hg)), cp(z_a), cp(z_b), cp(z_c))
    return y_ctx, y_lat


def hierarchical_moe(h, w_group, b_group, w_expert, b_expert, w_gate, w_up, w_down):
    f32 = jnp.float32
    bsz, t, d = h.shape
    x = h.reshape(-1, d)
    n = x.shape[0]
    g_logit = (x @ w_group).astype(f32) + b_group.astype(f32)
    g_idx = jnp.argmax(g_logit, axis=-1)
    g_prob = jnp.take_along_axis(jax.nn.softmax(g_logit, axis=-1), g_idx[:, None], axis=1)
    e_logit = ((x @ w_expert).astype(f32) + b_expert.astype(f32)).reshape(n, MOE_GROUPS, MOE_EXPERTS_PER_GROUP)
    e_logit = jnp.take_along_axis(e_logit, g_idx[:, None, None], axis=1)[:, 0]
    top_logit, top_idx = lax.top_k(e_logit, MOE_TOP_K)
    weight = (jax.nn.softmax(top_logit, axis=-1) * g_prob).reshape(-1)
    expert = (g_idx[:, None] * MOE_EXPERTS_PER_GROUP + top_idx).reshape(-1)
    token = jnp.repeat(jnp.arange(n, dtype=jnp.int32), MOE_TOP_K)
    n_assign = expert.shape[0]
    order = jnp.argsort(expert)
    expert_s, token_s, weight_s = expert[order], token[order], weight[order]
    counts = jnp.bincount(expert, length=MOE_EXPERTS)
    padded = (counts + MOE_BLOCK - 1) // MOE_BLOCK * MOE_BLOCK
    pad_end = jnp.cumsum(padded)
    start = jnp.cumsum(counts) - counts
    dest = pad_end[expert_s] - padded[expert_s] + jnp.arange(n_assign) - start[expert_s]
    n_blocks = -(-n_assign // MOE_BLOCK) + MOE_EXPERTS
    n_slots = n_blocks * MOE_BLOCK
    slot_token = jnp.zeros((n_slots,), jnp.int32).at[dest].set(token_s)
    slot_weight = jnp.zeros((n_slots,), f32).at[dest].set(weight_s)
    block_expert = jnp.minimum(jnp.searchsorted(pad_end, jnp.arange(n_blocks) * MOE_BLOCK, side='right'),
                               MOE_EXPERTS - 1)

    def run_block(args):
        tok, e = args
        xb = x[tok]
        return (jax.nn.silu(xb @ w_gate[e]) * (xb @ w_up[e])) @ w_down[e]

    y = lax.map(run_block, (slot_token.reshape(n_blocks, MOE_BLOCK), block_expert))
    out = jnp.zeros((n, d), f32).at[slot_token].add(y.reshape(n_slots, d).astype(f32) * slot_weight[:, None])
    return out.reshape(bsz, t, d).astype(h.dtype)


def setup_inputs(seed: int = 0) -> dict:
    key = jax.random.key(seed)
    ks = jax.random.split(key, 28)
    f32 = jnp.float32
    L, D = DEPTH, D_MODEL

    def nrm(k, shape, scale):
        return jax.random.normal(k, shape, f32) * scale

    def gain(k, shape):
        return 1.0 + nrm(k, shape, 0.02)

    return {
        'x': nrm(ks[0], (BATCH, SEQ, D), 1.0),
        'c': nrm(ks[1], (BATCH, D), 1.0),
        'ctx': nrm(ks[2], (BATCH, CTX_LEN, D), 1.0),
        'c_ctx': nrm(ks[3], (D,), 1.0),
        'w_mod': nrm(ks[4], (L, D, 6 * D), 0.5 * D ** -0.5),
        'b_mod': nrm(ks[5], (L, 6 * D), 0.02),
        'norm1_g': gain(ks[6], (L, D)),
        'w_in': nrm(ks[7], (L, D, IN_COLS), D ** -0.5),
        'mla_q_norm_g': gain(ks[8], (L, MLA_Q_RANK)),
        'mla_kv_norm_g': gain(ks[9], (L, MLA_KV_RANK)),
        'mla_w_uq': nrm(ks[10], (L, MLA_Q_RANK, MLA_HEADS, MLA_NOPE + MLA_ROPE), MLA_Q_RANK ** -0.5),
        'mla_w_ukv': nrm(ks[11], (L, MLA_KV_RANK, MLA_HEADS, MLA_NOPE + MLA_V), MLA_KV_RANK ** -0.5),
        'gla_w_a2': nrm(ks[12], (L, 2, GLA_GATE_RANK, GLA_HEADS * GLA_DK), GLA_GATE_RANK ** -0.5),
        'gla_b_a': nrm(ks[13], (L, 2, GLA_HEADS * GLA_DK), 0.1),
        'gla_onorm_g': gain(ks[14], (L, GLA_DV)),
        'hgrn_lb_logits': nrm(ks[15], (L, HGRN_HEADS * HGRN_DK), 0.5),
        'hgrn_onorm_g': gain(ks[16], (L, HGRN_DV)),
        'w_branch': nrm(ks[17], (L, N_BRANCHES, BRANCH_WIDTH, D), BRANCH_WIDTH ** -0.5),
        'w_out': nrm(ks[18], (L, D, D), D ** -0.5),
        'norm2_g': gain(ks[19], (L, D)),
        'moe_w_group': nrm(ks[20], (L, D, MOE_GROUPS), D ** -0.5),
        'moe_b_group': nrm(ks[21], (L, MOE_GROUPS), 0.01),
        'moe_w_expert': nrm(ks[22], (L, D, MOE_EXPERTS), D ** -0.5),
        'moe_b_expert': nrm(ks[23], (L, MOE_EXPERTS), 0.01),
        'moe_w_gate': nrm(ks[24], (L, MOE_EXPERTS, D, MOE_HIDDEN), D ** -0.5),
        'moe_w_up': nrm(ks[25], (L, MOE_EXPERTS, D, MOE_HIDDEN), D ** -0.5),
        'moe_w_down': nrm(ks[26], (L, MOE_EXPERTS, MOE_HIDDEN, D), MOE_HIDDEN ** -0.5),
        'final_norm_g': gain(ks[27], (D,)),
    }


def reference(x, c, ctx, c_ctx, w_mod, b_mod, norm1_g, w_in, mla_q_norm_g, mla_kv_norm_g,
              mla_w_uq, mla_w_ukv, gla_w_a2, gla_b_a, gla_onorm_g, hgrn_lb_logits, hgrn_onorm_g,
              w_branch, w_out, norm2_g, moe_w_group, moe_b_group, moe_w_expert, moe_b_expert,
              moe_w_gate, moe_w_up, moe_w_down, final_norm_g):
    n_ctx = ctx.shape[1]
    n_rows = x.shape[1] // GRID_W
    rope_cos, rope_sin = axial_rope_tables(n_rows)
    lb_cum = jnp.cumsum(jax.nn.softmax(hgrn_lb_logits.astype(jnp.float32), axis=0), axis=0)
    hgrn_lb = lb_cum - lb_cum[:1]
    cond_lat = jax.nn.silu(c)[:, None, :]
    cond_ctx = jax.nn.silu(c_ctx)[None, None, :]
    x_lat, x_ctx = x, ctx
    for layer in range(DEPTH):
        with_ctx = layer < DEPTH - 1
        m_lat = jnp.split(cond_lat @ w_mod[layer] + b_mod[layer], 6, axis=-1)
        m_ctx = jnp.split(cond_ctx @ w_mod[layer] + b_mod[layer], 6, axis=-1)
        h = jnp.concatenate([modulate(x_ctx, norm1_g[layer], m_ctx[0], m_ctx[1]),
                             modulate(x_lat, norm1_g[layer], m_lat[0], m_lat[1])], axis=1)
        y_ctx, y_lat = hybrid_mixer(h, n_ctx, rope_cos, rope_sin, with_ctx, w_in[layer],
                                    mla_q_norm_g[layer], mla_kv_norm_g[layer], mla_w_uq[layer],
                                    mla_w_ukv[layer], gla_w_a2[layer], gla_b_a[layer], gla_onorm_g[layer],
                                    hgrn_lb[layer], hgrn_onorm_g[layer], w_branch[layer], w_out[layer])
        x_lat = x_lat + m_lat[2] * y_lat
        h_lat = modulate(x_lat, norm2_g[layer], m_lat[3], m_lat[4])
        moe_args = (moe_w_group[layer], moe_b_group[layer], moe_w_expert[layer], moe_b_expert[layer],
                    moe_w_gate[layer], moe_w_up[layer], moe_w_down[layer])
        if with_ctx:
            x_ctx = x_ctx + m_ctx[2] * y_ctx
            h_ctx = modulate(x_ctx, norm2_g[layer], m_ctx[3], m_ctx[4])
            f = hierarchical_moe(jnp.concatenate([h_ctx, h_lat], axis=1), *moe_args)
            x_ctx = x_ctx + m_ctx[5] * f[:, :n_ctx]
            x_lat = x_lat + m_lat[5] * f[:, n_ctx:]
        else:
            x_lat = x_lat + m_lat[5] * hierarchical_moe(h_lat, *moe_args)
    return rmsnorm(x_lat, final_norm_g)
```

```python
import functools

import jax
import jax.numpy as jnp
from jax import lax
from jax.experimental import pallas as pl
from jax.experimental.pallas import tpu as pltpu

f32 = jnp.float32
bf16 = jnp.bfloat16

EPS = 1e-6
ROPE_THETA = 10000.0
GRID_W = 64

MLA_HEADS = 16
MLA_Q_RANK = 1024
MLA_KV_RANK = 512
MLA_NOPE = 128
MLA_ROPE = 64
MLA_V = 128
MLA_QK = 256
GLA_HEADS = 8
GLA_DK = 128
GLA_DV = 256
GLA_GATE_RANK = 16
GLA_TAU = 16.0
HGRN_HEADS = 16
HGRN_DK = 128
HGRN_DV = 128
BRANCH_WIDTH = 2048
MOE_GROUPS = 4
MOE_EPG = 4
MOE_EXPERTS = 16
MOE_TOP_K = 2
MOE_BLOCK = 256

SCAN_CHUNK = 64
SCAN_SUB = 16
LANES = 128

VMEM_LIMIT_BYTES = 56 * 1024 * 1024


def _params(n_axes):
    return pltpu.CompilerParams(dimension_semantics=("arbitrary",) * n_axes,
                                vmem_limit_bytes=VMEM_LIMIT_BYTES)


def _dot(a, b):
    return jnp.dot(a, b, preferred_element_type=f32)


def _dot_nt(a, b):
    return lax.dot_general(a, b, (((1,), (1,)), ((), ())), preferred_element_type=f32)


def _dot_tn(a, b):
    return lax.dot_general(a, b, (((0,), (0,)), ((), ())), preferred_element_type=f32)


def _log_sigmoid(x):
    return jnp.minimum(x, 0.0) - jnp.log1p(jnp.exp(-jnp.abs(x)))


def _sigmoid(x):
    return 1.0 / (1.0 + jnp.exp(-x))


def _mm_body(x_ref, w_ref, *rest, n_extra, epilogue):
    acc = _dot(x_ref[...].astype(bf16), w_ref[...].astype(bf16))
    epilogue(acc, rest[:n_extra], rest[n_extra:])


def _mm(x, w, epilogue, *, tm, tn, rows, ncols, outs, extras=(), x_blk0=0, name):
    k = x.shape[1]
    assert rows % tm == 0 and ncols % tn == 0 and w.shape[0] == k
    in_specs = [pl.BlockSpec((tm, k), lambda i, j: (i + x_blk0, 0)),
                pl.BlockSpec((k, tn), lambda i, j: (0, j))]
    in_specs += [pl.BlockSpec(bs, im) for _, bs, im in extras]
    res = pl.pallas_call(
        functools.partial(_mm_body, n_extra=len(extras), epilogue=epilogue),
        grid=(rows // tm, ncols // tn),
        in_specs=in_specs,
        out_specs=[pl.BlockSpec(bs, im) for _, bs, im in outs],
        out_shape=[s for s, _, _ in outs],
        compiler_params=_params(2),
        name=name,
    )(x, w, *[a for a, _, _ in extras])
    return res


def _ep_cast(acc, extras, outs):
    outs[0][...] = acc.astype(outs[0].dtype)


def _ep_bias(acc, extras, outs):
    outs[0][...] = (acc + extras[0][...]).astype(outs[0].dtype)


def _ep_rmsnorm(acc, extras, outs):
    ms = jnp.mean(acc * acc, axis=-1, keepdims=True)
    outs[0][...] = (acc * lax.rsqrt(ms + EPS) * extras[0][...]).astype(outs[0].dtype)


def _rotate(blk, cos_t, sin_t):
    return blk * cos_t + pltpu.roll(blk, 64, axis=1) * sin_t


def _ep_rope_small(acc, extras, outs):
    cos_t, sin_t = extras[0][...], extras[1][...]
    outs[0][...] = _rotate(acc[:, :LANES], cos_t, sin_t).astype(outs[0].dtype)
    outs[1][...] = acc[:, LANES:]


def _ep_rope_q(acc, extras, outs):
    cos_t, sin_t = extras[0][...], extras[1][...]
    o = outs[0]
    for hh in range(acc.shape[1] // MLA_QK):
        c0 = hh * MLA_QK
        o[:, c0:c0 + LANES] = acc[:, c0:c0 + LANES].astype(o.dtype)
        o[:, c0 + LANES:c0 + MLA_QK] = _rotate(acc[:, c0 + LANES:c0 + MLA_QK], cos_t, sin_t).astype(o.dtype)


def _ep_residual(acc, extras, outs):
    outs[0][...] = extras[0][...] + extras[1][0] * acc


def _mod_body(c_ref, w_ref, b_ref, o_ref):
    c = c_ref[...]
    a = (c * _sigmoid(c)).astype(bf16)
    o_ref[0] = _dot(a, w_ref[0].astype(bf16)) + b_ref[0]


def _mod_vectors(cond, w_mod, b_mod):
    n_l, d, d6 = w_mod.shape
    r = cond.shape[0]
    tn = min(512, d6)
    return pl.pallas_call(
        _mod_body,
        grid=(n_l, d6 // tn),
        in_specs=[pl.BlockSpec((r, d), lambda l, j: (0, 0)),
                  pl.BlockSpec((1, d, tn), lambda l, j: (l, 0, j)),
                  pl.BlockSpec((1, 1, tn), lambda l, j: (l, 0, j))],
        out_specs=pl.BlockSpec((1, r, tn), lambda l, j: (l, 0, j)),
        out_shape=jax.ShapeDtypeStruct((n_l, r, d6), f32),
        compiler_params=_params(2),
        name="mod_vectors",
    )(cond, w_mod, b_mod.reshape(n_l, 1, d6))


def _norm_mod_body(x_ref, g_ref, shift_ref, scale_ref, o_ref):
    x = x_ref[...]
    ms = jnp.mean(x * x, axis=-1, keepdims=True)
    y = x * lax.rsqrt(ms + EPS) * g_ref[...]
    o_ref[...] = (y * (1.0 + scale_ref[0]) + shift_ref[0]).astype(o_ref.dtype)


def _final_norm_body(x_ref, y0_ref, y1_ref, gate_ref, g_ref, o_ref):
    x = x_ref[...] + gate_ref[0] * (y0_ref[...] + y1_ref[...])
    ms = jnp.mean(x * x, axis=-1, keepdims=True)
    o_ref[...] = x * lax.rsqrt(ms + EPS) * g_ref[...]


def _combine_body(x_ref, y0_ref, y1_ref, gate_ref, o_ref):
    o_ref[...] = x_ref[...] + gate_ref[0] * (y0_ref[...] + y1_ref[...])


def _attn_lat_body(q_ref, knc_ref, knl_ref, vc_ref, vl_ref, krc_ref, krl_ref, o_ref,
                   k_scr, v_scr, *, n_ctx, tq, scale):
    k_scr[:n_ctx, :LANES] = knc_ref[...]
    k_scr[:n_ctx, LANES:] = krc_ref[...]
    k_scr[n_ctx:, :LANES] = knl_ref[...]
    k_scr[n_ctx:, LANES:] = krl_ref[...]
    v_scr[:n_ctx, :] = vc_ref[...]
    v_scr[n_ctx:, :] = vl_ref[...]

    def body(t, carry):
        r = pl.multiple_of(t * tq, tq)
        q = q_ref[pl.ds(r, tq), :]
        s = _dot_nt(q, k_scr[...]) * scale
        m = jnp.max(s, axis=-1, keepdims=True)
        p = jnp.exp(s - m)
        l = jnp.sum(p, axis=-1, keepdims=True)
        o = _dot(p.astype(bf16), v_scr[...])
        o_ref[pl.ds(r, tq), :] = (o / l).astype(o_ref.dtype)
        return carry

    lax.fori_loop(0, q_ref.shape[0] // tq, body, 0)


def _attn_ctx_body(q_ref, kn_ref, v_ref, kr_ref, o_ref, *, scale):
    k = jnp.concatenate([kn_ref[...], kr_ref[...]], axis=1)
    s = _dot_nt(q_ref[...], k) * scale
    m = jnp.max(s, axis=-1, keepdims=True)
    p = jnp.exp(s - m)
    l = jnp.sum(p, axis=-1, keepdims=True)
    o_ref[...] = (_dot(p.astype(bf16), v_ref[...]) / l).astype(o_ref.dtype)


def _attention(q, kv, kr, dims, with_ctx):
    b, s, n_ctx = dims["B"], dims["S"], dims["n_ctx"]
    nc, nl = dims["Nc"], dims["Nl"]
    lb0 = nc // s
    scale = (MLA_NOPE + MLA_ROPE) ** -0.5
    tq = min(256, s)
    o_lat = pl.pallas_call(
        functools.partial(_attn_lat_body, n_ctx=n_ctx, tq=tq, scale=scale),
        grid=(b, MLA_HEADS),
        in_specs=[
            pl.BlockSpec((s, MLA_QK), lambda i, h: (lb0 + i, h)),
            pl.BlockSpec((n_ctx, LANES), lambda i, h: (i, 2 * h)),
            pl.BlockSpec((s, LANES), lambda i, h: (lb0 + i, 2 * h)),
            pl.BlockSpec((n_ctx, LANES), lambda i, h: (i, 2 * h + 1)),
            pl.BlockSpec((s, LANES), lambda i, h: (lb0 + i, 2 * h + 1)),
            pl.BlockSpec((n_ctx, LANES), lambda i, h: (i, 0)),
            pl.BlockSpec((s, LANES), lambda i, h: (lb0 + i, 0)),
        ],
        out_specs=pl.BlockSpec((s, MLA_V), lambda i, h: (i, h)),
        out_shape=jax.ShapeDtypeStruct((nl, MLA_HEADS * MLA_V), bf16),
        scratch_shapes=[pltpu.VMEM((n_ctx + s, MLA_QK), bf16), pltpu.VMEM((n_ctx + s, MLA_V), bf16)],
        compiler_params=_params(2),
        name="mla_attention_latent",
    )(q, kv, kv, kv, kv, kr, kr)
    if not with_ctx:
        return None, o_lat
    o_ctx = pl.pallas_call(
        functools.partial(_attn_ctx_body, scale=scale),
        grid=(b, MLA_HEADS),
        in_specs=[
            pl.BlockSpec((n_ctx, MLA_QK), lambda i, h: (i, h)),
            pl.BlockSpec((n_ctx, LANES), lambda i, h: (i, 2 * h)),
            pl.BlockSpec((n_ctx, LANES), lambda i, h: (i, 2 * h + 1)),
            pl.BlockSpec((n_ctx, LANES), lambda i, h: (i, 0)),
        ],
        out_specs=pl.BlockSpec((n_ctx, MLA_V), lambda i, h: (i, h)),
        out_shape=jax.ShapeDtypeStruct((nc, MLA_HEADS * MLA_V), bf16),
        compiler_params=_params(2),
        name="mla_attention_context",
    )(q, kv, kv, kr)
    return o_ctx, o_lat


def _scan_chunk(q, k, v_bf, lg, st, reverse):
    c, sub = SCAN_CHUNK, SCAN_SUB
    row = lax.broadcasted_iota(jnp.int32, (c, c), 0)
    col = lax.broadcasted_iota(jnp.int32, (c, c), 1)
    ordered = (row <= col) if reverse else (row >= col)
    tri = jnp.where(ordered, 1.0, 0.0).astype(bf16)
    hi = lg.astype(bf16)
    lo = (lg - hi.astype(f32)).astype(bf16)
    cum = _dot(tri, hi) + _dot(tri, lo)
    tot = cum[0:1] if reverse else cum[c - 1:c]
    o = _dot_nt((q * jnp.exp(cum)).astype(bf16), st.astype(bf16))
    kd = (k * jnp.exp(tot - cum)).astype(bf16)
    st_new = st * jnp.exp(tot) + _dot_tn(v_bf, kd)

    col_s = lax.broadcasted_iota(jnp.int32, (sub, c), 1)
    blocks = []
    for blk in range(c // sub):
        r0 = blk * sub
        q_i, k_i, c_i = q[r0:r0 + sub], k[r0:r0 + sub], cum[r0:r0 + sub]
        a = jnp.zeros((sub, c), f32)
        for s in range(sub):
            e = jnp.exp(jnp.minimum(c_i - c_i[s:s + 1], 0.0))
            w = jnp.sum(q_i * e * k_i[s:s + 1], axis=-1, keepdims=True)
            a = jnp.where(col_s == r0 + s, w, a)
        has_prev = (blk < c // sub - 1) if reverse else (blk > 0)
        if has_prev:
            bnd = cum[r0 + sub:r0 + sub + 1] if reverse else cum[r0 - 1:r0]
            qp = (q_i * jnp.exp(c_i - bnd)).astype(bf16)
            kp = (k * jnp.exp(jnp.minimum(bnd - cum, 0.0))).astype(bf16)
            prev = (col_s >= r0 + sub) if reverse else (col_s < r0)
            a = jnp.where(prev, _dot_nt(qp, kp), a)
        blocks.append(a)
    a_full = jnp.where(ordered, jnp.concatenate(blocks, axis=0), 0.0)
    o = o + _dot(a_full.astype(bf16), v_bf)
    return o, st_new


def _rms_rows(x, g):
    ms = jnp.mean(x * x, axis=-1, keepdims=True)
    return x * lax.rsqrt(ms + EPS) * g


def _scan_body(*refs, mode, n_ctx, seq):
    if mode == "gla":
        (q_c, q_l, k_c, k_l, v_c, v_l, r_c, r_l, ga_c, ga_l, wa_ref, ba_ref, g_ref,
         o_c, o_l, st_ref, of_ref, ob_ref) = refs
    else:
        (q_c, q_l, v_c, v_l, r_c, r_l, ff_c, ff_l, fb_c, fb_l, la_ref, lc_ref, oml_ref, g_ref,
         o_c, o_l, st_ref, of_ref, ob_ref) = refs
    c = SCAN_CHUNK
    st_ref[...] = jnp.zeros_like(st_ref)
    o_dir = (of_ref, ob_ref)

    def chunk_inputs(seg, r, d):
        q = (q_l if seg else q_c)[pl.ds(r, c), :].astype(f32)
        v = (v_l if seg else v_c)[pl.ds(r, c), :]
        if mode == "gla":
            k = (k_l if seg else k_c)[pl.ds(r, c), :].astype(f32)
            ga = (ga_l if seg else ga_c)[pl.ds(r, c), :].astype(bf16)
            logit = _dot(ga, wa_ref[0, d]) + ba_ref[0, d]
            lg = _log_sigmoid(logit) / GLA_TAU
            q = q * (GLA_DK ** -0.5)
        else:
            fref = ((ff_l if seg else ff_c), (fb_l if seg else fb_c))[d]
            fl = fref[pl.ds(r, c), :]
            la = la_ref[0]
            bb = lc_ref[0] + _log_sigmoid(fl)
            lg = jnp.maximum(la, bb) + jnp.log1p(jnp.exp(-jnp.abs(la - bb)))
            k = oml_ref[0] * _sigmoid(-fl)
        return q, k, v, lg

    for seg, (n_rows, base) in enumerate(((n_ctx, 0), (seq, n_ctx))):
        n_chunks = n_rows // c

        def body(i, carry, seg=seg, n_chunks=n_chunks, base=base):
            for d in range(2):
                ci = (n_chunks - 1 - i) if d else i
                r = pl.multiple_of(ci * c, c)
                q, k, v, lg = chunk_inputs(seg, r, d)
                o, st_new = _scan_chunk(q, k, v, lg, st_ref[d], reverse=bool(d))
                st_ref[d] = st_new
                o_dir[d][pl.ds(base + r, c), :] = o
            return carry

        lax.fori_loop(0, n_chunks, body, 0)

    tr = min(256, n_ctx, seq)
    g = g_ref[...]
    for seg, (n_rows, base) in enumerate(((n_ctx, 0), (seq, n_ctx))):
        r_ref = r_l if seg else r_c
        o_ref = o_l if seg else o_c

        def post(t, carry, base=base, r_ref=r_ref, o_ref=o_ref):
            r = pl.multiple_of(t * tr, tr)
            o = of_ref[pl.ds(base + r, tr), :] + ob_ref[pl.ds(base + r, tr), :]
            gate = r_ref[pl.ds(r, tr), :].astype(f32)
            if mode == "gla":
                res = gate * _sigmoid(gate) * _rms_rows(o, g)
            else:
                res = _rms_rows(o * _sigmoid(gate), g)
            o_ref[pl.ds(r, tr), :] = res.astype(o_ref.dtype)
            return carry

        lax.fori_loop(0, n_rows // tr, post, 0)


def _scan_call(mode, tok_inputs, const_inputs, dims, n_heads, dv):
    b, s, n_ctx = dims["B"], dims["S"], dims["n_ctx"]
    nc, nl = dims["Nc"], dims["Nl"]
    lb0 = nc // s
    in_specs, args = [], []
    for arr, width, colf in tok_inputs:
        in_specs.append(pl.BlockSpec((n_ctx, width), lambda i, h, colf=colf: (i, colf(h))))
        in_specs.append(pl.BlockSpec((s, width), lambda i, h, colf=colf: (lb0 + i, colf(h))))
        args += [arr, arr]
    for arr, bs, im in const_inputs:
        in_specs.append(pl.BlockSpec(bs, im))
        args.append(arr)
    return pl.pallas_call(
        functools.partial(_scan_body, mode=mode, n_ctx=n_ctx, seq=s),
        grid=(b, n_heads),
        in_specs=in_specs,
        out_specs=[pl.BlockSpec((n_ctx, dv), lambda i, h: (i, h)),
                   pl.BlockSpec((s, dv), lambda i, h: (i, h))],
        out_shape=[jax.ShapeDtypeStruct((nc, n_heads * dv), bf16),
                   jax.ShapeDtypeStruct((nl, n_heads * dv), bf16)],
        scratch_shapes=[pltpu.VMEM((2, dv, LANES), f32),
                        pltpu.VMEM((n_ctx + s, dv), f32),
                        pltpu.VMEM((n_ctx + s, dv), f32)],
        compiler_params=_params(2),
        name="scan_" + mode,
    )(*args)


def _merge_body(oa_ref, ob_ref, oc_ref, w_ref, za_ref, zb_ref, zc_ref, m_ref):
    acc = _sigmoid(za_ref[...].astype(f32)) * _dot(oa_ref[...], w_ref[0])
    acc += _sigmoid(zb_ref[...].astype(f32)) * _dot(ob_ref[...], w_ref[1])
    acc += _sigmoid(zc_ref[...].astype(f32)) * _dot(oc_ref[...], w_ref[2])
    m_ref[...] = acc.astype(m_ref.dtype)


def _merge(o_a, o_b, o_c, w_branch, big, z_col0, rows, row0, tm, d):
    tn = min(512, d)
    assert rows % tm == 0 and row0 % tm == 0 and z_col0 % tn == 0
    zb0 = z_col0 // tn
    nzb = d // tn
    rb0 = row0 // tm
    o_spec = pl.BlockSpec((tm, BRANCH_WIDTH), lambda i, j: (i, 0))
    return pl.pallas_call(
        _merge_body,
        grid=(rows // tm, d // tn),
        in_specs=[o_spec, o_spec, o_spec,
                  pl.BlockSpec((3, BRANCH_WIDTH, tn), lambda i, j: (0, 0, j)),
                  pl.BlockSpec((tm, tn), lambda i, j: (rb0 + i, zb0 + j)),
                  pl.BlockSpec((tm, tn), lambda i, j: (rb0 + i, zb0 + nzb + j)),
                  pl.BlockSpec((tm, tn), lambda i, j: (rb0 + i, zb0 + 2 * nzb + j))],
        out_specs=pl.BlockSpec((tm, tn), lambda i, j: (i, j)),
        out_shape=jax.ShapeDtypeStruct((rows, d), bf16),
        compiler_params=_params(2),
        name="branch_merge",
    )(o_a, o_b, o_c, w_branch, big, big, big)


def _moe_body(be_ref, nused_ref, dst_ref, w_ref, h_hbm, wg_ref, wu_ref, wd_ref, y_hbm,
              xbuf, ybuf, sem_in, sem_out):
    i = pl.program_id(0)
    blk = MOE_BLOCK

    @pl.when(i < nused_ref[0])
    def _():
        base = i * blk

        def issue(r, carry):
            tok = jnp.maximum(dst_ref[base + r], 0) // MOE_TOP_K
            pltpu.make_async_copy(h_hbm.at[pl.ds(tok, 1)], xbuf.at[pl.ds(r, 1)], sem_in).start()
            return carry

        lax.fori_loop(0, blk, issue, 0)

        def wait_in(r, carry):
            pltpu.make_async_copy(h_hbm.at[pl.ds(0, 1)], xbuf.at[pl.ds(r, 1)], sem_in).wait()
            return carry

        lax.fori_loop(0, blk, wait_in, 0)

        x = xbuf[...].astype(bf16)
        gate = _dot(x, wg_ref[0])
        up = _dot(x, wu_ref[0])
        act = (gate * _sigmoid(gate) * up).astype(bf16)
        ybuf[...] = _dot(act, wd_ref[0]) * w_ref[0]

        def put(r, carry):
            dst = dst_ref[base + r]

            @pl.when(dst >= 0)
            def _():
                pltpu.make_async_copy(ybuf.at[pl.ds(r, 1)], y_hbm.at[pl.ds(dst, 1)], sem_out).start()

            return carry

        lax.fori_loop(0, blk, put, 0)

        def wait_out(r, carry):
            dst = dst_ref[base + r]

            @pl.when(dst >= 0)
            def _():
                pltpu.make_async_copy(ybuf.at[pl.ds(r, 1)], y_hbm.at[pl.ds(0, 1)], sem_out).wait()

            return carry

        lax.fori_loop(0, blk, wait_out, 0)


def _moe_experts(h2, slot_dst, slot_w, block_expert, n_used, wg, wu, wd):
    rows, d = h2.shape
    n_blocks = block_expert.shape[0]
    hid = wg.shape[-1]
    grid_spec = pltpu.PrefetchScalarGridSpec(
        num_scalar_prefetch=3,
        grid=(n_blocks,),
        in_specs=[
            pl.BlockSpec((1, MOE_BLOCK, 1), lambda i, be, nu, ds: (i, 0, 0)),
            pl.BlockSpec(memory_space=pl.ANY),
            pl.BlockSpec((1, d, hid), lambda i, be, nu, ds: (be[i], 0, 0)),
            pl.BlockSpec((1, d, hid), lambda i, be, nu, ds: (be[i], 0, 0)),
            pl.BlockSpec((1, hid, d), lambda i, be, nu, ds: (be[i], 0, 0)),
        ],
        out_specs=pl.BlockSpec(memory_space=pl.ANY),
        scratch_shapes=[pltpu.VMEM((MOE_BLOCK, d), f32), pltpu.VMEM((MOE_BLOCK, d), f32),
                        pltpu.SemaphoreType.DMA(()), pltpu.SemaphoreType.DMA(())],
    )
    return pl.pallas_call(
        _moe_body,
        grid_spec=grid_spec,
        out_shape=jax.ShapeDtypeStruct((rows * MOE_TOP_K, d), f32),
        compiler_params=_params(1),
        name="moe_experts",
    )(block_expert, n_used, slot_dst, slot_w.reshape(n_blocks, MOE_BLOCK, 1), h2, wg, wu, wd)


def _moe_dispatch(logits, rows):
    g_logit = logits[:, :MOE_GROUPS]
    g_idx = jnp.argmax(g_logit, axis=-1)
    g_prob = jnp.take_along_axis(jax.nn.softmax(g_logit, axis=-1), g_idx[:, None], axis=1)
    e_logit = logits[:, MOE_GROUPS:MOE_GROUPS + MOE_EXPERTS].reshape(rows, MOE_GROUPS, MOE_EPG)
    e_logit = jnp.take_along_axis(e_logit, g_idx[:, None, None], axis=1)[:, 0]
    top_logit, top_idx = lax.top_k(e_logit, MOE_TOP_K)
    weight = (jax.nn.softmax(top_logit, axis=-1) * g_prob).reshape(-1)
    expert = (g_idx[:, None] * MOE_EPG + top_idx).reshape(-1).astype(jnp.int32)
    n_assign = rows * MOE_TOP_K
    onehot = (expert[:, None] == jnp.arange(MOE_EXPERTS, dtype=jnp.int32)[None, :]).astype(jnp.int32)
    csum = jnp.cumsum(onehot, axis=0)
    rank = jnp.take_along_axis(csum, expert[:, None], axis=1)[:, 0] - 1
    counts = csum[-1]
    padded = (counts + MOE_BLOCK - 1) // MOE_BLOCK * MOE_BLOCK
    pad_end = jnp.cumsum(padded)
    dest = (pad_end - padded)[expert] + rank
    n_blocks = -(-n_assign // MOE_BLOCK) + MOE_EXPERTS
    n_slots = n_blocks * MOE_BLOCK
    slot_dst = jnp.full((n_slots,), -1, jnp.int32).at[dest].set(jnp.arange(n_assign, dtype=jnp.int32))
    slot_w = jnp.zeros((n_slots,), f32).at[dest].set(weight)
    block_expert = jnp.minimum(
        jnp.searchsorted(pad_end, jnp.arange(n_blocks, dtype=jnp.int32) * MOE_BLOCK, side="right"),
        MOE_EXPERTS - 1).astype(jnp.int32)
    n_used = (pad_end[-1:] // MOE_BLOCK).astype(jnp.int32)
    return slot_dst, slot_w, block_expert, n_used


def _rope_swap_cols(w):
    qd = MLA_ROPE // 4
    return jnp.concatenate([-w[..., qd:2 * qd], w[..., :qd], -w[..., 3 * qd:], w[..., 2 * qd:3 * qd]], axis=-1)


def _rope_tables(seq, tm):
    qd = MLA_ROPE // 4
    n_rows = seq // GRID_W
    inv_freq = ROPE_THETA ** (-jnp.arange(qd, dtype=f32) / qd)
    row = jnp.repeat(jnp.arange(n_rows, dtype=f32), GRID_W)
    col = jnp.tile(jnp.arange(GRID_W, dtype=f32), n_rows)
    ar, ac = row[:, None] * inv_freq, col[:, None] * inv_freq
    zeros = jnp.zeros((seq, LANES - MLA_ROPE), f32)
    cos_t = jnp.concatenate([jnp.cos(ar), jnp.cos(ar), jnp.cos(ac), jnp.cos(ac), zeros], axis=-1)
    sin_t = jnp.concatenate([jnp.sin(ar), jnp.sin(ar), jnp.sin(ac), jnp.sin(ac), zeros], axis=-1)
    ident = jnp.concatenate([jnp.ones((tm, MLA_ROPE), f32), jnp.zeros((tm, LANES - MLA_ROPE), f32)], axis=-1)
    return (jnp.concatenate([ident, cos_t], axis=0),
            jnp.concatenate([jnp.zeros((tm, LANES), f32), sin_t], axis=0))


def _layer(xg, modv, lw, dims, with_ctx, final_g):
    b, s, n_ctx, d = dims["B"], dims["S"], dims["n_ctx"], dims["D"]
    nc, nl, n = dims["Nc"], dims["Nl"], dims["N"]
    tile = dims["tile"]

    def mod_blk(which, tm, row0):
        def im(i, *_):
            g0 = i * tm + row0
            cond = jnp.where(g0 < nc, b, jnp.maximum(g0 - nc, 0) // s)
            return (cond * 6 + which, 0, 0)
        return im

    def norm_mod(x, g, which_shift, which_scale, rows, row0, dtype):
        tm = min(256, tile)
        return pl.pallas_call(
            _norm_mod_body,
            grid=(rows // tm,),
            in_specs=[pl.BlockSpec((tm, d), lambda i: (i, 0)),
                      pl.BlockSpec((1, d), lambda i: (0, 0)),
                      pl.BlockSpec((1, 1, d), mod_blk(which_shift, tm, row0)),
                      pl.BlockSpec((1, 1, d), mod_blk(which_scale, tm, row0))],
            out_specs=pl.BlockSpec((tm, d), lambda i: (i, 0)),
            out_shape=jax.ShapeDtypeStruct((rows, d), dtype),
            compiler_params=_params(1),
            name="norm_modulate",
        )(x, g.reshape(1, d), modv, modv)

    h = norm_mod(xg, lw["norm1_g"], 0, 1, n, 0, bf16)

    tm = min(1024, tile)
    row_tile = lambda i, j: (i, 0)

    def full_out(cols, dtype, tn):
        return (jax.ShapeDtypeStruct((n, cols), dtype), (tm, tn), lambda i, j: (i, j))

    cqn, = _mm(h, lw["w_cq"], _ep_rmsnorm, tm=tm, tn=MLA_Q_RANK, rows=n, ncols=MLA_Q_RANK,
               extras=[(lw["q_norm_g"], (1, MLA_Q_RANK), lambda i, j: (0, 0))],
               outs=[full_out(MLA_Q_RANK, bf16, MLA_Q_RANK)], name="in_proj_cq")
    ckvn, = _mm(h, lw["w_ckv"], _ep_rmsnorm, tm=tm, tn=MLA_KV_RANK, rows=n, ncols=MLA_KV_RANK,
                extras=[(lw["kv_norm_g"], (1, MLA_KV_RANK), lambda i, j: (0, 0))],
                outs=[full_out(MLA_KV_RANK, bf16, MLA_KV_RANK)], name="in_proj_ckv")

    cos_t, sin_t = _rope_tables(s, tm)

    def rope_blk(i, j):
        g0 = i * tm
        return (jnp.where(g0 < nc, 0, 1 + (jnp.maximum(g0 - nc, 0) % s) // tm), 0)

    rope_extras = [(cos_t, (tm, LANES), rope_blk), (sin_t, (tm, LANES), rope_blk)]
    kr, ga = _mm(h, lw["w_small"], _ep_rope_small, tm=tm, tn=2 * LANES, rows=n, ncols=2 * LANES,
                 extras=rope_extras,
                 outs=[(jax.ShapeDtypeStruct((n, LANES), bf16), (tm, LANES), row_tile),
                       (jax.ShapeDtypeStruct((n, LANES), f32), (tm, LANES), row_tile)],
                 name="in_proj_small")
    n_big = lw["w_big"].shape[1]
    tn_big = min(1024, d)
    big, = _mm(h, lw["w_big"], _ep_cast, tm=tm, tn=tn_big, rows=n, ncols=n_big,
               outs=[full_out(n_big, bf16, tn_big)], name="in_proj_big")
    n_hf = lw["w_hf"].shape[1]
    hf, = _mm(h, lw["w_hf"], _ep_cast, tm=tm, tn=1024, rows=n, ncols=n_hf,
              outs=[full_out(n_hf, f32, 1024)], name="in_proj_hf")

    qw = MLA_HEADS * MLA_QK
    q, = _mm(cqn, lw["w_uq"], _ep_rope_q, tm=tm, tn=1024, rows=n, ncols=qw,
             extras=rope_extras, outs=[full_out(qw, bf16, 1024)], name="mla_q_up")
    kv, = _mm(ckvn, lw["w_ukv"], _ep_cast, tm=tm, tn=1024, rows=n, ncols=qw,
              outs=[full_out(qw, bf16, 1024)], name="mla_kv_up")
    o_a_ctx, o_a_lat = _attention(q, kv, kr, dims, with_ctx)

    gk0 = GLA_HEADS * GLA_DK
    gv0 = 2 * gk0
    gr0 = gv0 + GLA_HEADS * GLA_DV
    hq0 = gr0 + GLA_HEADS * GLA_DV
    hi0 = hq0 + HGRN_HEADS * HGRN_DK
    hg0 = hi0 + HGRN_HEADS * HGRN_DV
    z0 = hg0 + HGRN_HEADS * HGRN_DV
    o_b_ctx, o_b_lat = _scan_call(
        "gla",
        [(big, GLA_DK, lambda hh: hh), (big, GLA_DK, lambda hh: gk0 // GLA_DK + hh),
         (big, GLA_DV, lambda hh: gv0 // GLA_DV + hh), (big, GLA_DV, lambda hh: gr0 // GLA_DV + hh),
         (ga, LANES, lambda hh: 0)],
        [(lw["gla_wa"], (1, 2, LANES, GLA_DK), lambda i, hh: (hh, 0, 0, 0)),
         (lw["gla_ba"], (1, 2, 1, GLA_DK), lambda i, hh: (hh, 0, 0, 0)),
         (lw["gla_onorm_g"], (1, GLA_DV), lambda i, hh: (0, 0))],
        dims, GLA_HEADS, GLA_DV)

    o_c_ctx, o_c_lat = _scan_call(
        "hgrn",
        [(big, HGRN_DK, lambda hh: hq0 // HGRN_DK + hh), (big, HGRN_DV, lambda hh: hi0 // HGRN_DV + hh),
         (big, HGRN_DV, lambda hh: hg0 // HGRN_DV + hh),
         (hf, HGRN_DK, lambda hh: hh), (hf, HGRN_DK, lambda hh: HGRN_HEADS + hh)],
        [(lw["hgrn_log_lb"], (1, 1, HGRN_DK), lambda i, hh: (hh, 0, 0)),
         (lw["hgrn_log_1mlb"], (1, 1, HGRN_DK), lambda i, hh: (hh, 0, 0)),
         (lw["hgrn_1mlb"], (1, 1, HGRN_DK), lambda i, hh: (hh, 0, 0)),
         (lw["hgrn_onorm_g"], (1, HGRN_DV), lambda i, hh: (0, 0))],
        dims, HGRN_HEADS, HGRN_DV)

    if with_ctx:
        o_a = jnp.concatenate([o_a_ctx, o_a_lat], axis=0)
        o_b = jnp.concatenate([o_b_ctx, o_b_lat], axis=0)
        o_c = jnp.concatenate([o_c_ctx, o_c_lat], axis=0)
        rows, row0 = n, 0
    else:
        o_a, o_b, o_c = o_a_lat, o_b_lat, o_c_lat
        rows, row0 = nl, nc
    tmo = min(512, tile)
    m = _merge(o_a, o_b, o_c, lw["w_branch"], big, z0, rows, row0, tmo, d)
    tno = min(1024, d)
    x1, = _mm(m, lw["w_out"], _ep_residual, tm=tmo, tn=tno, rows=rows, ncols=d,
              extras=[(xg, (tmo, tno), lambda i, j: (i + row0 // tmo, j)),
                      (modv, (1, 1, tno), lambda i, j: mod_blk(2, tmo, row0)(i)[:1] + (0, j))],
              outs=[(jax.ShapeDtypeStruct((rows, d), f32), (tmo, tno), lambda i, j: (i, j))],
              name="out_proj_residual")

    h2 = norm_mod(x1, lw["norm2_g"], 3, 4, rows, row0, f32)
    tmr = min(512, tile)
    logits, = _mm(h2, lw["w_router"], _ep_bias, tm=tmr, tn=LANES, rows=rows, ncols=LANES,
                  extras=[(lw["b_router"], (1, LANES), lambda i, j: (0, 0))],
                  outs=[(jax.ShapeDtypeStruct((rows, LANES), f32), (tmr, LANES), row_tile)],
                  name="moe_router")
    slot_dst, slot_w, block_expert, n_used = _moe_dispatch(logits, rows)
    y2 = _moe_experts(h2, slot_dst, slot_w, block_expert, n_used,
                      lw["moe_w_gate"], lw["moe_w_up"], lw["moe_w_down"])
    y2 = y2.reshape(rows, MOE_TOP_K * d)

    tmc = min(256, tile)
    in_specs = [pl.BlockSpec((tmc, d), lambda i: (i, 0)),
                pl.BlockSpec((tmc, d), lambda i: (i, 0)),
                pl.BlockSpec((tmc, d), lambda i: (i, 1)),
                pl.BlockSpec((1, 1, d), mod_blk(5, tmc, row0))]
    args = [x1, y2, y2, modv]
    if final_g is None:
        body, name = _combine_body, "moe_combine"
    else:
        body, name = _final_norm_body, "moe_combine_final_norm"
        in_specs.append(pl.BlockSpec((1, d), lambda i: (0, 0)))
        args.append(final_g.reshape(1, d))
    return pl.pallas_call(
        body,
        grid=(rows // tmc,),
        in_specs=in_specs,
        out_specs=pl.BlockSpec((tmc, d), lambda i: (i, 0)),
        out_shape=jax.ShapeDtypeStruct((rows, d), f32),
        compiler_params=_params(1),
        name=name,
    )(*args)


def _layer_weights(layer, d, w_in, mla_q_norm_g, mla_kv_norm_g, mla_w_uq, mla_w_ukv, gla_w_a2, gla_b_a,
                   gla_onorm_g, hgrn_lb, hgrn_onorm_g, w_branch, w_out, norm1_g, norm2_g,
                   moe_w_group, moe_b_group, moe_w_expert, moe_b_expert, moe_w_gate, moe_w_up, moe_w_down):
    wi = w_in[layer]
    widths = (MLA_Q_RANK, MLA_KV_RANK, MLA_ROPE,
              GLA_HEADS * GLA_DK, GLA_HEADS * GLA_DK, GLA_HEADS * GLA_DV, GLA_HEADS * GLA_DV,
              GLA_GATE_RANK, GLA_GATE_RANK,
              HGRN_HEADS * HGRN_DK, HGRN_HEADS * HGRN_DV, HGRN_HEADS * HGRN_DK, HGRN_HEADS * HGRN_DK,
              HGRN_HEADS * HGRN_DV, d, d, d)
    offs = [0]
    for w in widths:
        offs.append(offs[-1] + w)
    assert offs[-1] == wi.shape[1]
    col = lambda a, e: wi[:, offs[a]:offs[e]]
    w_kr = col(2, 3)
    w_small = jnp.concatenate(
        [w_kr, _rope_swap_cols(w_kr), col(7, 9),
         jnp.zeros((d, 2 * LANES - 2 * MLA_ROPE - 2 * GLA_GATE_RANK), f32)], axis=-1)
    w_big = jnp.concatenate([col(3, 7), col(9, 11), col(13, 14), col(14, 17)], axis=-1)
    uq = mla_w_uq[layer]
    uq_rope = uq[..., MLA_NOPE:]
    w_uq = jnp.concatenate([uq[..., :MLA_NOPE], uq_rope, _rope_swap_cols(uq_rope)], axis=-1)
    wa = jnp.zeros((2, LANES, GLA_HEADS * GLA_DK), f32)
    wa = wa.at[0, :GLA_GATE_RANK].set(gla_w_a2[layer, 0])
    wa = wa.at[1, GLA_GATE_RANK:2 * GLA_GATE_RANK].set(gla_w_a2[layer, 1])
    wa = wa.reshape(2, LANES, GLA_HEADS, GLA_DK).transpose(2, 0, 1, 3)
    ba = gla_b_a[layer].reshape(2, GLA_HEADS, 1, GLA_DK).transpose(1, 0, 2, 3)
    lb = hgrn_lb[layer].reshape(HGRN_HEADS, 1, HGRN_DK)
    w_router = jnp.concatenate(
        [moe_w_group[layer], moe_w_expert[layer], jnp.zeros((d, LANES - MOE_GROUPS - MOE_EXPERTS), f32)], axis=-1)
    b_router = jnp.concatenate(
        [moe_b_group[layer], moe_b_expert[layer], jnp.zeros((LANES - MOE_GROUPS - MOE_EXPERTS,), f32)])
    return {
        "norm1_g": norm1_g[layer], "norm2_g": norm2_g[layer],
        "w_cq": col(0, 1).astype(bf16), "w_ckv": col(1, 2).astype(bf16),
        "w_small": w_small.astype(bf16), "w_big": w_big.astype(bf16), "w_hf": col(11, 13).astype(bf16),
        "q_norm_g": mla_q_norm_g[layer].reshape(1, -1), "kv_norm_g": mla_kv_norm_g[layer].reshape(1, -1),
        "w_uq": w_uq.reshape(MLA_Q_RANK, MLA_HEADS * MLA_QK).astype(bf16),
        "w_ukv": mla_w_ukv[layer].reshape(MLA_KV_RANK, MLA_HEADS * (MLA_NOPE + MLA_V)).astype(bf16),
        "gla_wa": wa.astype(bf16), "gla_ba": ba, "gla_onorm_g": gla_onorm_g[layer].reshape(1, -1),
        "hgrn_log_lb": jnp.log(lb), "hgrn_log_1mlb": jnp.log1p(-lb), "hgrn_1mlb": 1.0 - lb,
        "hgrn_onorm_g": hgrn_onorm_g[layer].reshape(1, -1),
        "w_branch": w_branch[layer].astype(bf16), "w_out": w_out[layer].astype(bf16),
        "w_router": w_router.astype(bf16), "b_router": b_router.reshape(1, LANES),
        "moe_w_gate": moe_w_gate[layer].astype(bf16), "moe_w_up": moe_w_up[layer].astype(bf16),
        "moe_w_down": moe_w_down[layer].astype(bf16),
    }


def kernel(x, c, ctx, c_ctx, w_mod, b_mod, norm1_g, w_in, mla_q_norm_g, mla_kv_norm_g, mla_w_uq, mla_w_ukv, gla_w_a2, gla_b_a, gla_onorm_g, hgrn_lb_logits, hgrn_onorm_g, w_branch, w_out, norm2_g, moe_w_group, moe_b_group, moe_w_expert, moe_b_expert, moe_w_gate, moe_w_up, moe_w_down, final_norm_g):
    b, s, d = x.shape
    n_ctx = ctx.shape[1]
    depth = w_mod.shape[0]
    nc, nl = b * n_ctx, b * s
    assert nc % s == 0 and s % GRID_W == 0 and n_ctx % SCAN_CHUNK == 0 and s % SCAN_CHUNK == 0
    tile = 1
    while nc % (tile * 2) == 0 and s % (tile * 2) == 0:
        tile *= 2
    dims = {"B": b, "S": s, "n_ctx": n_ctx, "D": d, "Nc": nc, "Nl": nl, "N": nc + nl, "tile": tile}

    n_cond = -(-(b + 1) // 8) * 8
    cond = jnp.concatenate([c, c_ctx[None, :], jnp.zeros((n_cond - b - 1, d), f32)], axis=0)
    mod_all = _mod_vectors(cond, w_mod, b_mod)

    lb_cum = jnp.cumsum(jax.nn.softmax(hgrn_lb_logits.astype(f32), axis=0), axis=0)
    hgrn_lb = lb_cum - lb_cum[:1]

    xg = jnp.concatenate([ctx.reshape(nc, d), x.reshape(nl, d)], axis=0)
    for layer in range(depth):
        with_ctx = layer < depth - 1
        lw = _layer_weights(layer, d, w_in, mla_q_norm_g, mla_kv_norm_g, mla_w_uq, mla_w_ukv, gla_w_a2,
                            gla_b_a, gla_onorm_g, hgrn_lb, hgrn_onorm_g, w_branch, w_out, norm1_g, norm2_g,
                            moe_w_group, moe_b_group, moe_w_expert, moe_b_expert, moe_w_gate, moe_w_up,
                            moe_w_down)
        modv = mod_all[layer, :b + 1].reshape((b + 1) * 6, 1, d)
        xg = _layer(xg, modv, lw, dims, with_ctx, None if with_ctx else final_norm_g)
    return xg.reshape(b, s, d)
```

```python
import functools

import jax
import jax.numpy as jnp
from jax import lax
from jax.experimental import pallas as pl
from jax.experimental.pallas import tpu as pltpu

f32 = jnp.float32
bf16 = jnp.bfloat16

EPS = 1e-6
LOG2E = 1.4426950408889634
ROPE_THETA = 10000.0
GRID_W = 64

MLA_HEADS = 16
MLA_Q_RANK = 1024
MLA_KV_RANK = 512
MLA_NOPE = 128
MLA_ROPE = 64
MLA_V = 128
MLA_QK = 256
GLA_HEADS = 8
GLA_DK = 128
GLA_DV = 256
GLA_GATE_RANK = 16
GLA_TAU = 16.0
HGRN_HEADS = 16
HGRN_DK = 128
HGRN_DV = 128
BRANCH_WIDTH = 2048
MOE_GROUPS = 4
MOE_EPG = 4
MOE_EXPERTS = 16
MOE_TOP_K = 2
MOE_BLOCK = 256

SCAN_CHUNK = 64
LANES = 128

VMEM_LIMIT_BYTES = 56 * 1024 * 1024


def _params(n_axes):
    return pltpu.CompilerParams(dimension_semantics=("arbitrary",) * n_axes,
                                vmem_limit_bytes=VMEM_LIMIT_BYTES)


def _dot(a, b):
    return jnp.dot(a, b, preferred_element_type=f32)


def _dot_nt(a, b):
    return lax.dot_general(a, b, (((1,), (1,)), ((), ())), preferred_element_type=f32)


def _dot_tn(a, b):
    return lax.dot_general(a, b, (((0,), (0,)), ((), ())), preferred_element_type=f32)


def _sigmoid(x):
    return 1.0 / (1.0 + jnp.exp(-x))


def _mm_body(x_ref, w_ref, *rest, n_extra, epilogue):
    acc = _dot(x_ref[...].astype(bf16), w_ref[...].astype(bf16))
    epilogue(acc, rest[:n_extra], rest[n_extra:])


def _mm(x, w, epilogue, *, tm, tn, rows, ncols, outs, extras=(), x_blk0=0, name):
    k = x.shape[1]
    assert rows % tm == 0 and ncols % tn == 0 and w.shape[0] == k
    in_specs = [pl.BlockSpec((tm, k), lambda i, j: (i + x_blk0, 0)),
                pl.BlockSpec((k, tn), lambda i, j: (0, j))]
    in_specs += [pl.BlockSpec(bs, im) for _, bs, im in extras]
    res = pl.pallas_call(
        functools.partial(_mm_body, n_extra=len(extras), epilogue=epilogue),
        grid=(rows // tm, ncols // tn),
        in_specs=in_specs,
        out_specs=[pl.BlockSpec(bs, im) for _, bs, im in outs],
        out_shape=[s for s, _, _ in outs],
        compiler_params=_params(2),
        name=name,
    )(x, w, *[a for a, _, _ in extras])
    return res


def _ep_cast(acc, extras, outs):
    outs[0][...] = acc.astype(outs[0].dtype)


def _ep_bias(acc, extras, outs):
    outs[0][...] = (acc + extras[0][...]).astype(outs[0].dtype)


def _ep_rmsnorm(acc, extras, outs):
    ms = jnp.mean(acc * acc, axis=-1, keepdims=True)
    outs[0][...] = (acc * lax.rsqrt(ms + EPS) * extras[0][...]).astype(outs[0].dtype)


def _rotate(blk, cos_t, sin_t):
    return blk * cos_t + pltpu.roll(blk, 64, axis=1) * sin_t


def _ep_rope_small(acc, extras, outs):
    cos_t, sin_t = extras[0][...], extras[1][...]
    outs[0][...] = _rotate(acc[:, :LANES], cos_t, sin_t).astype(outs[0].dtype)
    outs[1][...] = acc[:, LANES:]


def _ep_rope_q(acc, extras, outs):
    cos_t, sin_t = extras[0][...], extras[1][...]
    o = outs[0]
    for hh in range(acc.shape[1] // MLA_QK):
        c0 = hh * MLA_QK
        o[:, c0:c0 + LANES] = acc[:, c0:c0 + LANES].astype(o.dtype)
        o[:, c0 + LANES:c0 + MLA_QK] = _rotate(acc[:, c0 + LANES:c0 + MLA_QK], cos_t, sin_t).astype(o.dtype)


def _ep_residual(acc, extras, outs):
    outs[0][...] = extras[0][...] + extras[1][0] * acc


def _mod_body(c_ref, w_ref, b_ref, o_ref):
    c = c_ref[...]
    a = (c * _sigmoid(c)).astype(bf16)
    o_ref[0] = _dot(a, w_ref[0].astype(bf16)) + b_ref[0]


def _mod_vectors(cond, w_mod, b_mod):
    n_l, d, d6 = w_mod.shape
    r = cond.shape[0]
    tn = min(512, d6)
    return pl.pallas_call(
        _mod_body,
        grid=(n_l, d6 // tn),
        in_specs=[pl.BlockSpec((r, d), lambda l, j: (0, 0)),
                  pl.BlockSpec((1, d, tn), lambda l, j: (l, 0, j)),
                  pl.BlockSpec((1, 1, tn), lambda l, j: (l, 0, j))],
        out_specs=pl.BlockSpec((1, r, tn), lambda l, j: (l, 0, j)),
        out_shape=jax.ShapeDtypeStruct((n_l, r, d6), f32),
        compiler_params=_params(2),
        name="mod_vectors",
    )(cond, w_mod, b_mod.reshape(n_l, 1, d6))


def _norm_mod_body(x_ref, g_ref, shift_ref, scale_ref, o_ref):
    x = x_ref[...]
    ms = jnp.mean(x * x, axis=-1, keepdims=True)
    y = x * lax.rsqrt(ms + EPS) * g_ref[...]
    o_ref[...] = (y * (1.0 + scale_ref[0]) + shift_ref[0]).astype(o_ref.dtype)


def _moe_sum(x_ref, y0_ref, y1_ref, w_ref, gate_ref):
    w = w_ref[...]
    return x_ref[...] + gate_ref[0] * (y0_ref[...] * w[:, 0:1] + y1_ref[...] * w[:, 1:2])


def _final_norm_body(x_ref, y0_ref, y1_ref, w_ref, gate_ref, g_ref, o_ref):
    x = _moe_sum(x_ref, y0_ref, y1_ref, w_ref, gate_ref)
    ms = jnp.mean(x * x, axis=-1, keepdims=True)
    o_ref[...] = x * lax.rsqrt(ms + EPS) * g_ref[...]


def _combine_body(x_ref, y0_ref, y1_ref, w_ref, gate_ref, o_ref):
    o_ref[...] = _moe_sum(x_ref, y0_ref, y1_ref, w_ref, gate_ref)


def _attn_lat_body(q_ref, knc_ref, knl_ref, vc_ref, vl_ref, krc_ref, krl_ref, o_ref,
                   k_scr, v_scr, *, n_ctx, tq, scale):
    k_scr[:n_ctx, :LANES] = knc_ref[...]
    k_scr[:n_ctx, LANES:] = krc_ref[...]
    k_scr[n_ctx:, :LANES] = knl_ref[...]
    k_scr[n_ctx:, LANES:] = krl_ref[...]
    v_scr[:n_ctx, :] = vc_ref[...]
    v_scr[n_ctx:, :] = vl_ref[...]

    def body(t, carry):
        r = pl.multiple_of(t * tq, tq)
        q = q_ref[pl.ds(r, tq), :]
        s = _dot_nt(q, k_scr[...]) * scale
        m = jnp.max(s, axis=-1, keepdims=True)
        p = jnp.exp(s - m)
        l = jnp.sum(p, axis=-1, keepdims=True)
        o = _dot(p.astype(bf16), v_scr[...])
        o_ref[pl.ds(r, tq), :] = (o / l).astype(o_ref.dtype)
        return carry

    lax.fori_loop(0, q_ref.shape[0] // tq, body, 0)


def _attn_ctx_body(q_ref, kn_ref, v_ref, kr_ref, o_ref, *, scale):
    k = jnp.concatenate([kn_ref[...], kr_ref[...]], axis=1)
    s = _dot_nt(q_ref[...], k) * scale
    m = jnp.max(s, axis=-1, keepdims=True)
    p = jnp.exp(s - m)
    l = jnp.sum(p, axis=-1, keepdims=True)
    o_ref[...] = (_dot(p.astype(bf16), v_ref[...]) / l).astype(o_ref.dtype)


def _attention(q, kv, kr, dims, with_ctx):
    b, s, n_ctx = dims["B"], dims["S"], dims["n_ctx"]
    nc, nl = dims["Nc"], dims["Nl"]
    lb0 = nc // s
    scale = (MLA_NOPE + MLA_ROPE) ** -0.5
    tq = min(256, s)
    o_lat = pl.pallas_call(
        functools.partial(_attn_lat_body, n_ctx=n_ctx, tq=tq, scale=scale),
        grid=(b, MLA_HEADS),
        in_specs=[
            pl.BlockSpec((s, MLA_QK), lambda i, h: (lb0 + i, h)),
            pl.BlockSpec((n_ctx, LANES), lambda i, h: (i, 2 * h)),
            pl.BlockSpec((s, LANES), lambda i, h: (lb0 + i, 2 * h)),
            pl.BlockSpec((n_ctx, LANES), lambda i, h: (i, 2 * h + 1)),
            pl.BlockSpec((s, LANES), lambda i, h: (lb0 + i, 2 * h + 1)),
            pl.BlockSpec((n_ctx, LANES), lambda i, h: (i, 0)),
            pl.BlockSpec((s, LANES), lambda i, h: (lb0 + i, 0)),
        ],
        out_specs=pl.BlockSpec((s, MLA_V), lambda i, h: (i, h)),
        out_shape=jax.ShapeDtypeStruct((nl, MLA_HEADS * MLA_V), bf16),
        scratch_shapes=[pltpu.VMEM((n_ctx + s, MLA_QK), bf16), pltpu.VMEM((n_ctx + s, MLA_V), bf16)],
        compiler_params=_params(2),
        name="mla_attention_latent",
    )(q, kv, kv, kv, kv, kr, kr)
    if not with_ctx:
        return None, o_lat
    o_ctx = pl.pallas_call(
        functools.partial(_attn_ctx_body, scale=scale),
        grid=(b, MLA_HEADS),
        in_specs=[
            pl.BlockSpec((n_ctx, MLA_QK), lambda i, h: (i, h)),
            pl.BlockSpec((n_ctx, LANES), lambda i, h: (i, 2 * h)),
            pl.BlockSpec((n_ctx, LANES), lambda i, h: (i, 2 * h + 1)),
            pl.BlockSpec((n_ctx, LANES), lambda i, h: (i, 0)),
        ],
        out_specs=pl.BlockSpec((n_ctx, MLA_V), lambda i, h: (i, h)),
        out_shape=jax.ShapeDtypeStruct((nc, MLA_HEADS * MLA_V), bf16),
        compiler_params=_params(2),
        name="mla_attention_context",
    )(q, kv, kv, kr)
    return o_ctx, o_lat


def _scan_levels(reverse):
    c = SCAN_CHUNK
    row = lax.broadcasted_iota(jnp.int32, (c, c), 0)
    col = lax.broadcasted_iota(jnp.int32, (c, c), 1)
    earlier = (row < col) if reverse else (row > col)
    x = row ^ col
    x = x | (x >> 1)
    x = x | (x >> 2)
    x = x | (x >> 4)
    top_bit = (x + 1) >> 1
    return jnp.where(earlier, top_bit, jnp.where(row == col, c, 0))


def _scan_chunks(items):
    c = SCAN_CHUNK
    kw = items[0][3].shape[1]
    row_k = lax.broadcasted_iota(jnp.int32, (c, kw), 0)
    sub8 = lax.broadcasted_iota(jnp.int32, (8, kw), 0)

    cums = []
    for q, k, v_bf, lg2, st, lvl, reverse in items:
        cum = lg2
        for s in (1, 2, 4):
            if reverse:
                cum = cum + jnp.where(row_k < c - s, pltpu.roll(cum, c - s, axis=0), 0.0)
            else:
                cum = cum + jnp.where(row_k >= s, pltpu.roll(cum, s, axis=0), 0.0)
        for s in (8, 16, 32):
            if reverse:
                cum = jnp.concatenate([cum[:c - s] + cum[s:], cum[c - s:]], axis=0)
            else:
                cum = jnp.concatenate([cum[:s], cum[s:] + cum[:c - s]], axis=0)
        cums.append(cum)

    operands = []
    for (q, k, v_bf, lg2, st, lvl, reverse), cum in zip(items, cums):
        tot = cum[0:1] if reverse else cum[c - 1:c]
        qe = (q * jnp.exp2(cum)).astype(bf16)
        kd = (k * jnp.exp2(tot - cum)).astype(bf16)
        levels = []
        h = c // 2
        while h >= 1:
            later = ((row_k & h) == 0) if reverse else ((row_k & h) != 0)
            if h >= 8:
                groups = []
                for r0 in range(0, c, 8):
                    blk = r0 // (2 * h) * (2 * h)
                    bnd = cum[blk + h - (0 if reverse else 1):blk + h + (1 if reverse else 0)]
                    if ((r0 - blk) >= h) != reverse:
                        groups.append(q[r0:r0 + 8] * jnp.exp2(cum[r0:r0 + 8] - bnd))
                    else:
                        groups.append(k[r0:r0 + 8] * jnp.exp2(bnd - cum[r0:r0 + 8]))
                m = jnp.concatenate(groups, axis=0)
            elif h > 1:
                groups = []
                for r0 in range(0, c, 8):
                    bnds = [jnp.broadcast_to(cum[b + h - (0 if reverse else 1):b + h + (1 if reverse else 0)],
                                             (8, kw)) for b in range(r0, r0 + 8, 2 * h)]
                    groups.append(bnds[0] if len(bnds) == 1 else jnp.where(sub8 < 4, bnds[0], bnds[1]))
                d = cum - jnp.concatenate(groups, axis=0)
                m = jnp.where(later, q, k) * jnp.exp2(-jnp.abs(d))
            else:
                m = jnp.where(later, q * jnp.exp2(lg2), k)
            levels.append((h, m.astype(bf16)))
            h //= 2
        operands.append((tot, qe, kd, st.astype(bf16), levels))

    products = []
    for (q, k, v_bf, lg2, st, lvl, reverse), (tot, qe, kd, st_bf, levels) in zip(items, operands):
        products.append((_dot_nt(qe, st_bf), _dot_tn(v_bf, kd), [(h, _dot_nt(m, m)) for h, m in levels]))

    scores = []
    for (q, k, v_bf, lg2, st, lvl, reverse), (o_inter, upd, ps) in zip(items, products):
        a = jnp.where(lvl == c, jnp.sum(q * k, axis=-1, keepdims=True), 0.0)
        for h, p in ps:
            a = jnp.where(lvl == h, p, a)
        scores.append(a.astype(bf16))

    results = []
    for (q, k, v_bf, lg2, st, lvl, reverse), (tot, _, _, _, _), (o_inter, upd, _), a in zip(
            items, operands, products, scores):
        results.append((o_inter + _dot(a, v_bf), st * jnp.exp2(tot) + upd))
    return results


def _rms_rows(x, g):
    ms = jnp.mean(x * x, axis=-1, keepdims=True)
    return x * lax.rsqrt(ms + EPS) * g


def _scan_body(*refs, mode, n_ctx, seq):
    if mode == "gla":
        (q_c, q_l, k_c, k_l, v_c, v_l, r_c, r_l, ga_c, ga_l, wa_ref, ba_ref, g_ref,
         o_c, o_l, sf_ref, sb_ref, of_ref, ob_ref, lvl_ref) = refs
    else:
        (q_c, q_l, v_c, v_l, r_c, r_l, ff_c, ff_l, fb_c, fb_l, la_ref, lc_ref, oml_ref, g_ref,
         o_c, o_l, sf_ref, sb_ref, of_ref, ob_ref, lvl_ref) = refs
    c = SCAN_CHUNK
    st_dir = (sf_ref, sb_ref)
    o_dir = (of_ref, ob_ref)
    for d in range(2):
        st_dir[d][...] = jnp.zeros_like(st_dir[d])
        lvl_ref[d] = _scan_levels(bool(d))

    def chunk_inputs(seg, r, d):
        q = (q_l if seg else q_c)[pl.ds(r, c), :].astype(f32)
        v = (v_l if seg else v_c)[pl.ds(r, c), :]
        if mode == "gla":
            k = (k_l if seg else k_c)[pl.ds(r, c), :].astype(f32)
            ga = (ga_l if seg else ga_c)[pl.ds(r, c), :].astype(bf16)
            x2 = (_dot(ga, wa_ref[0, d]) + ba_ref[0, d]) * LOG2E
            lg2 = (jnp.minimum(x2, 0.0) - jnp.log2(1.0 + jnp.exp2(-jnp.abs(x2)))) * (1.0 / GLA_TAU)
            q = q * (GLA_DK ** -0.5)
        else:
            fref = ((ff_l if seg else ff_c), (fb_l if seg else fb_c))[d]
            f2 = fref[pl.ds(r, c), :] * LOG2E
            ls2 = jnp.minimum(f2, 0.0) - jnp.log2(1.0 + jnp.exp2(-jnp.abs(f2)))
            la2 = la_ref[0]
            bb2 = lc_ref[0] + ls2
            lg2 = jnp.maximum(la2, bb2) + jnp.log2(1.0 + jnp.exp2(-jnp.abs(la2 - bb2)))
            k = oml_ref[0] * jnp.exp2(ls2 - f2)
        return q, k, v, lg2

    for seg, (n_rows, base) in enumerate(((n_ctx, 0), (seq, n_ctx))):
        n_chunks = n_rows // c

        def body(i, carry, seg=seg, n_chunks=n_chunks, base=base):
            rows, items = [], []
            for d in range(2):
                ci = (n_chunks - 1 - i) if d else i
                r = pl.multiple_of(ci * c, c)
                rows.append(r)
                items.append(chunk_inputs(seg, r, d) + (st_dir[d][...], lvl_ref[d], bool(d)))
            for d, (o, st_new) in enumerate(_scan_chunks(items)):
                st_dir[d][...] = st_new
                o_dir[d][pl.ds(base + rows[d], c), :] = o
            return carry

        lax.fori_loop(0, n_chunks, body, 0, unroll=2)

    tr = min(256, n_ctx, seq)
    g = g_ref[...]
    for seg, (n_rows, base) in enumerate(((n_ctx, 0), (seq, n_ctx))):
        r_ref = r_l if seg else r_c
        o_ref = o_l if seg else o_c

        def post(t, carry, base=base, r_ref=r_ref, o_ref=o_ref):
            r = pl.multiple_of(t * tr, tr)
            o = of_ref[pl.ds(base + r, tr), :] + ob_ref[pl.ds(base + r, tr), :]
            gate = r_ref[pl.ds(r, tr), :].astype(f32)
            if mode == "gla":
                res = gate * _sigmoid(gate) * _rms_rows(o, g)
            else:
                res = _rms_rows(o * _sigmoid(gate), g)
            o_ref[pl.ds(r, tr), :] = res.astype(o_ref.dtype)
            return carry

        lax.fori_loop(0, n_rows // tr, post, 0)


def _scan_call(mode, tok_inputs, const_inputs, dims, n_heads, dv):
    b, s, n_ctx = dims["B"], dims["S"], dims["n_ctx"]
    nc, nl = dims["Nc"], dims["Nl"]
    lb0 = nc // s
    in_specs, args = [], []
    for arr, width, colf in tok_inputs:
        in_specs.append(pl.BlockSpec((n_ctx, width), lambda i, h, colf=colf: (i, colf(h))))
        in_specs.append(pl.BlockSpec((s, width), lambda i, h, colf=colf: (lb0 + i, colf(h))))
        args += [arr, arr]
    for arr, bs, im in const_inputs:
        in_specs.append(pl.BlockSpec(bs, im))
        args.append(arr)
    return pl.pallas_call(
        functools.partial(_scan_body, mode=mode, n_ctx=n_ctx, seq=s),
        grid=(b, n_heads),
        in_specs=in_specs,
        out_specs=[pl.BlockSpec((n_ctx, dv), lambda i, h: (i, h)),
                   pl.BlockSpec((s, dv), lambda i, h: (i, h))],
        out_shape=[jax.ShapeDtypeStruct((nc, n_heads * dv), bf16),
                   jax.ShapeDtypeStruct((nl, n_heads * dv), bf16)],
        scratch_shapes=[pltpu.VMEM((dv, LANES), f32),
                        pltpu.VMEM((dv, LANES), f32),
                        pltpu.VMEM((n_ctx + s, dv), f32),
                        pltpu.VMEM((n_ctx + s, dv), f32),
                        pltpu.VMEM((2, SCAN_CHUNK, SCAN_CHUNK), jnp.int32)],
        compiler_params=_params(2),
        name="scan_" + mode,
    )(*args)


def _merge_body(oa_ref, ob_ref, oc_ref, w_ref, za_ref, zb_ref, zc_ref, m_ref):
    acc = _sigmoid(za_ref[...].astype(f32)) * _dot(oa_ref[...], w_ref[0])
    acc += _sigmoid(zb_ref[...].astype(f32)) * _dot(ob_ref[...], w_ref[1])
    acc += _sigmoid(zc_ref[...].astype(f32)) * _dot(oc_ref[...], w_ref[2])
    m_ref[...] = acc.astype(m_ref.dtype)


def _merge(o_a, o_b, o_c, w_branch, big, z_col0, rows, row0, tm, d):
    tn = min(512, d)
    assert rows % tm == 0 and row0 % tm == 0 and z_col0 % tn == 0
    zb0 = z_col0 // tn
    nzb = d // tn
    rb0 = row0 // tm
    o_spec = pl.BlockSpec((tm, BRANCH_WIDTH), lambda i, j: (i, 0))
    return pl.pallas_call(
        _merge_body,
        grid=(rows // tm, d // tn),
        in_specs=[o_spec, o_spec, o_spec,
                  pl.BlockSpec((3, BRANCH_WIDTH, tn), lambda i, j: (0, 0, j)),
                  pl.BlockSpec((tm, tn), lambda i, j: (rb0 + i, zb0 + j)),
                  pl.BlockSpec((tm, tn), lambda i, j: (rb0 + i, zb0 + nzb + j)),
                  pl.BlockSpec((tm, tn), lambda i, j: (rb0 + i, zb0 + 2 * nzb + j))],
        out_specs=pl.BlockSpec((tm, tn), lambda i, j: (i, j)),
        out_shape=jax.ShapeDtypeStruct((rows, d), bf16),
        compiler_params=_params(2),
        name="branch_merge",
    )(o_a, o_b, o_c, w_branch, big, big, big)


def _moe_body(be_ref, nused_ref, dst_ref, h_hbm, wg_ref, wu_ref, wd_ref, y_hbm,
              xbuf, ybuf, sem_in, sem_out, *, rows):
    i = pl.program_id(0)
    blk = MOE_BLOCK

    @pl.when(i == 0)
    def _():
        ybuf[...] = jnp.zeros_like(ybuf)
        spare = pltpu.make_async_copy(ybuf, y_hbm.at[pl.ds(MOE_TOP_K * rows, blk)], sem_out)
        spare.start()
        spare.wait()

    @pl.when(i < nused_ref[0])
    def _():
        base = i * blk

        def issue(r, carry):
            tok = jnp.maximum(dst_ref[base + r], 0) >> 1
            pltpu.make_async_copy(h_hbm.at[pl.ds(tok, 1)], xbuf.at[pl.ds(r, 1)], sem_in).start()
            return carry

        lax.fori_loop(0, blk, issue, 0, unroll=8)
        pltpu.make_async_copy(h_hbm.at[pl.ds(0, blk)], xbuf, sem_in).wait()

        x = xbuf[...].astype(bf16)
        gate = _dot(x, wg_ref[0])
        up = _dot(x, wu_ref[0])
        act = (gate * _sigmoid(gate) * up).astype(bf16)
        ybuf[...] = _dot(act, wd_ref[0])

        def put(r, carry):
            a = dst_ref[base + r]
            row = jnp.where(a >= 0, (a & 1) * rows + (a >> 1), MOE_TOP_K * rows + r)
            pltpu.make_async_copy(ybuf.at[pl.ds(r, 1)], y_hbm.at[pl.ds(row, 1)], sem_out).start()
            return carry

        lax.fori_loop(0, blk, put, 0, unroll=8)
        pltpu.make_async_copy(ybuf, y_hbm.at[pl.ds(0, blk)], sem_out).wait()


def _moe_experts(h2, slot_dst, block_expert, n_used, wg, wu, wd):
    rows, d = h2.shape
    n_blocks = block_expert.shape[0]
    hid = wg.shape[-1]
    assert MOE_TOP_K == 2
    grid_spec = pltpu.PrefetchScalarGridSpec(
        num_scalar_prefetch=3,
        grid=(n_blocks,),
        in_specs=[
            pl.BlockSpec(memory_space=pl.ANY),
            pl.BlockSpec((1, d, hid), lambda i, be, nu, ds: (be[i], 0, 0)),
            pl.BlockSpec((1, d, hid), lambda i, be, nu, ds: (be[i], 0, 0)),
            pl.BlockSpec((1, hid, d), lambda i, be, nu, ds: (be[i], 0, 0)),
        ],
        out_specs=pl.BlockSpec(memory_space=pl.ANY),
        scratch_shapes=[pltpu.VMEM((MOE_BLOCK, d), f32), pltpu.VMEM((MOE_BLOCK, d), f32),
                        pltpu.SemaphoreType.DMA(()), pltpu.SemaphoreType.DMA(())],
    )
    return pl.pallas_call(
        functools.partial(_moe_body, rows=rows),
        grid_spec=grid_spec,
        out_shape=jax.ShapeDtypeStruct((rows * MOE_TOP_K + MOE_BLOCK, d), f32),
        compiler_params=_params(1),
        name="moe_experts",
    )(block_expert, n_used, slot_dst, h2, wg, wu, wd)


def _argmax_cols(cols):
    idx = jnp.zeros(cols[0].shape, jnp.int32)
    best = cols[0]
    for j in range(1, len(cols)):
        better = cols[j] > best
        idx = jnp.where(better, j, idx)
        best = jnp.where(better, cols[j], best)
    return idx, best


def _moe_route(logits):
    g = [logits[:, j] for j in range(MOE_GROUPS)]
    g_idx, g_max = _argmax_cols(g)
    g_prob = 1.0 / sum(jnp.exp(gj - g_max) for gj in g)
    e = []
    for j in range(MOE_EPG):
        ej = logits[:, MOE_GROUPS + j]
        for grp in range(1, MOE_GROUPS):
            ej = jnp.where(g_idx == grp, logits[:, MOE_GROUPS + grp * MOE_EPG + j], ej)
        e.append(ej)
    i1, v1 = _argmax_cols(e)
    i2, v2 = _argmax_cols([jnp.where(i1 == j, -jnp.inf, e[j]) for j in range(MOE_EPG)])
    t = jnp.exp(v2 - v1)
    weight = jnp.stack([1.0 / (1.0 + t), t / (1.0 + t)], axis=-1) * g_prob[:, None]
    expert = g_idx[:, None] * MOE_EPG + jnp.stack([i1, i2], axis=-1)
    return expert, weight


def _moe_dispatch(expert):
    expert = expert.reshape(-1)
    n_assign = expert.shape[0]
    ids = jnp.arange(MOE_EXPERTS, dtype=jnp.int32)
    onehot = (expert[:, None] == ids[None, :]).astype(jnp.int32)
    csum = jnp.cumsum(onehot, axis=0)
    rank = jnp.sum(onehot * csum, axis=1) - 1
    counts = csum[-1]
    padded = (counts + MOE_BLOCK - 1) // MOE_BLOCK * MOE_BLOCK
    pad_end = jnp.cumsum(padded)
    dest = jnp.sum(onehot * (pad_end - padded)[None, :], axis=1) + rank
    n_blocks = -(-n_assign // MOE_BLOCK) + MOE_EXPERTS
    slot_dst = jnp.full((n_blocks * MOE_BLOCK,), -1, jnp.int32).at[dest].set(
        jnp.arange(n_assign, dtype=jnp.int32))
    starts = jnp.arange(n_blocks, dtype=jnp.int32) * MOE_BLOCK
    block_expert = jnp.minimum(jnp.sum((pad_end[None, :] <= starts[:, None]).astype(jnp.int32), axis=1),
                               MOE_EXPERTS - 1)
    n_used = pad_end[-1:] // MOE_BLOCK
    return slot_dst, block_expert.astype(jnp.int32), n_used.astype(jnp.int32)


def _rope_swap_cols(w):
    qd = MLA_ROPE // 4
    return jnp.concatenate([-w[..., qd:2 * qd], w[..., :qd], -w[..., 3 * qd:], w[..., 2 * qd:3 * qd]], axis=-1)


def _rope_tables(seq, tm):
    qd = MLA_ROPE // 4
    n_rows = seq // GRID_W
    inv_freq = ROPE_THETA ** (-jnp.arange(qd, dtype=f32) / qd)
    row = jnp.repeat(jnp.arange(n_rows, dtype=f32), GRID_W)
    col = jnp.tile(jnp.arange(GRID_W, dtype=f32), n_rows)
    ar, ac = row[:, None] * inv_freq, col[:, None] * inv_freq
    zeros = jnp.zeros((seq, LANES - MLA_ROPE), f32)
    cos_t = jnp.concatenate([jnp.cos(ar), jnp.cos(ar), jnp.cos(ac), jnp.cos(ac), zeros], axis=-1)
    sin_t = jnp.concatenate([jnp.sin(ar), jnp.sin(ar), jnp.sin(ac), jnp.sin(ac), zeros], axis=-1)
    ident = jnp.concatenate([jnp.ones((tm, MLA_ROPE), f32), jnp.zeros((tm, LANES - MLA_ROPE), f32)], axis=-1)
    return (jnp.concatenate([ident, cos_t], axis=0),
            jnp.concatenate([jnp.zeros((tm, LANES), f32), sin_t], axis=0))


def _layer(xg, modv, lw, dims, with_ctx, final_g):
    b, s, n_ctx, d = dims["B"], dims["S"], dims["n_ctx"], dims["D"]
    nc, nl, n = dims["Nc"], dims["Nl"], dims["N"]
    tile = dims["tile"]

    def mod_blk(which, tm, row0):
        def im(i, *_):
            g0 = i * tm + row0
            cond = jnp.where(g0 < nc, b, jnp.maximum(g0 - nc, 0) // s)
            return (cond * 6 + which, 0, 0)
        return im

    def norm_mod(x, g, which_shift, which_scale, rows, row0, dtype):
        tm = min(256, tile)
        return pl.pallas_call(
            _norm_mod_body,
            grid=(rows // tm,),
            in_specs=[pl.BlockSpec((tm, d), lambda i: (i, 0)),
                      pl.BlockSpec((1, d), lambda i: (0, 0)),
                      pl.BlockSpec((1, 1, d), mod_blk(which_shift, tm, row0)),
                      pl.BlockSpec((1, 1, d), mod_blk(which_scale, tm, row0))],
            out_specs=pl.BlockSpec((tm, d), lambda i: (i, 0)),
            out_shape=jax.ShapeDtypeStruct((rows, d), dtype),
            compiler_params=_params(1),
            name="norm_modulate",
        )(x, g.reshape(1, d), modv, modv)

    h = norm_mod(xg, lw["norm1_g"], 0, 1, n, 0, bf16)

    tm = min(1024, tile)
    row_tile = lambda i, j: (i, 0)

    def full_out(cols, dtype, tn):
        return (jax.ShapeDtypeStruct((n, cols), dtype), (tm, tn), lambda i, j: (i, j))

    cqn, = _mm(h, lw["w_cq"], _ep_rmsnorm, tm=tm, tn=MLA_Q_RANK, rows=n, ncols=MLA_Q_RANK,
               extras=[(lw["q_norm_g"], (1, MLA_Q_RANK), lambda i, j: (0, 0))],
               outs=[full_out(MLA_Q_RANK, bf16, MLA_Q_RANK)], name="in_proj_cq")
    ckvn, = _mm(h, lw["w_ckv"], _ep_rmsnorm, tm=tm, tn=MLA_KV_RANK, rows=n, ncols=MLA_KV_RANK,
                extras=[(lw["kv_norm_g"], (1, MLA_KV_RANK), lambda i, j: (0, 0))],
                outs=[full_out(MLA_KV_RANK, bf16, MLA_KV_RANK)], name="in_proj_ckv")

    cos_t, sin_t = _rope_tables(s, tm)

    def rope_blk(i, j):
        g0 = i * tm
        return (jnp.where(g0 < nc, 0, 1 + (jnp.maximum(g0 - nc, 0) % s) // tm), 0)

    rope_extras = [(cos_t, (tm, LANES), rope_blk), (sin_t, (tm, LANES), rope_blk)]
    kr, ga = _mm(h, lw["w_small"], _ep_rope_small, tm=tm, tn=2 * LANES, rows=n, ncols=2 * LANES,
                 extras=rope_extras,
                 outs=[(jax.ShapeDtypeStruct((n, LANES), bf16), (tm, LANES), row_tile),
                       (jax.ShapeDtypeStruct((n, LANES), f32), (tm, LANES), row_tile)],
                 name="in_proj_small")
    n_big = lw["w_big"].shape[1]
    tn_big = min(1024, d)
    big, = _mm(h, lw["w_big"], _ep_cast, tm=tm, tn=tn_big, rows=n, ncols=n_big,
               outs=[full_out(n_big, bf16, tn_big)], name="in_proj_big")
    n_hf = lw["w_hf"].shape[1]
    hf, = _mm(h, lw["w_hf"], _ep_cast, tm=tm, tn=1024, rows=n, ncols=n_hf,
              outs=[full_out(n_hf, f32, 1024)], name="in_proj_hf")

    qw = MLA_HEADS * MLA_QK
    q, = _mm(cqn, lw["w_uq"], _ep_rope_q, tm=tm, tn=1024, rows=n, ncols=qw,
             extras=rope_extras, outs=[full_out(qw, bf16, 1024)], name="mla_q_up")
    kv, = _mm(ckvn, lw["w_ukv"], _ep_cast, tm=tm, tn=1024, rows=n, ncols=qw,
              outs=[full_out(qw, bf16, 1024)], name="mla_kv_up")
    o_a_ctx, o_a_lat = _attention(q, kv, kr, dims, with_ctx)

    gk0 = GLA_HEADS * GLA_DK
    gv0 = 2 * gk0
    gr0 = gv0 + GLA_HEADS * GLA_DV
    hq0 = gr0 + GLA_HEADS * GLA_DV
    hi0 = hq0 + HGRN_HEADS * HGRN_DK
    hg0 = hi0 + HGRN_HEADS * HGRN_DV
    z0 = hg0 + HGRN_HEADS * HGRN_DV
    o_b_ctx, o_b_lat = _scan_call(
        "gla",
        [(big, GLA_DK, lambda hh: hh), (big, GLA_DK, lambda hh: gk0 // GLA_DK + hh),
         (big, GLA_DV, lambda hh: gv0 // GLA_DV + hh), (big, GLA_DV, lambda hh: gr0 // GLA_DV + hh),
         (ga, LANES, lambda hh: 0)],
        [(lw["gla_wa"], (1, 2, LANES, GLA_DK), lambda i, hh: (hh, 0, 0, 0)),
         (lw["gla_ba"], (1, 2, 1, GLA_DK), lambda i, hh: (hh, 0, 0, 0)),
         (lw["gla_onorm_g"], (1, GLA_DV), lambda i, hh: (0, 0))],
        dims, GLA_HEADS, GLA_DV)

    o_c_ctx, o_c_lat = _scan_call(
        "hgrn",
        [(big, HGRN_DK, lambda hh: hq0 // HGRN_DK + hh), (big, HGRN_DV, lambda hh: hi0 // HGRN_DV + hh),
         (big, HGRN_DV, lambda hh: hg0 // HGRN_DV + hh),
         (hf, HGRN_DK, lambda hh: hh), (hf, HGRN_DK, lambda hh: HGRN_HEADS + hh)],
        [(lw["hgrn_log_lb"], (1, 1, HGRN_DK), lambda i, hh: (hh, 0, 0)),
         (lw["hgrn_log_1mlb"], (1, 1, HGRN_DK), lambda i, hh: (hh, 0, 0)),
         (lw["hgrn_1mlb"], (1, 1, HGRN_DK), lambda i, hh: (hh, 0, 0)),
         (lw["hgrn_onorm_g"], (1, HGRN_DV), lambda i, hh: (0, 0))],
        dims, HGRN_HEADS, HGRN_DV)

    if with_ctx:
        o_a = jnp.concatenate([o_a_ctx, o_a_lat], axis=0)
        o_b = jnp.concatenate([o_b_ctx, o_b_lat], axis=0)
        o_c = jnp.concatenate([o_c_ctx, o_c_lat], axis=0)
        rows, row0 = n, 0
    else:
        o_a, o_b, o_c = o_a_lat, o_b_lat, o_c_lat
        rows, row0 = nl, nc
    tmo = min(512, tile)
    m = _merge(o_a, o_b, o_c, lw["w_branch"], big, z0, rows, row0, tmo, d)
    tno = min(1024, d)
    x1, = _mm(m, lw["w_out"], _ep_residual, tm=tmo, tn=tno, rows=rows, ncols=d,
              extras=[(xg, (tmo, tno), lambda i, j: (i + row0 // tmo, j)),
                      (modv, (1, 1, tno), lambda i, j: mod_blk(2, tmo, row0)(i)[:1] + (0, j))],
              outs=[(jax.ShapeDtypeStruct((rows, d), f32), (tmo, tno), lambda i, j: (i, j))],
              name="out_proj_residual")

    h2 = norm_mod(x1, lw["norm2_g"], 3, 4, rows, row0, f32)
    tmr = min(512, tile)
    logits, = _mm(h2, lw["w_router"], _ep_bias, tm=tmr, tn=LANES, rows=rows, ncols=LANES,
                  extras=[(lw["b_router"], (1, LANES), lambda i, j: (0, 0))],
                  outs=[(jax.ShapeDtypeStruct((rows, LANES), f32), (tmr, LANES), row_tile)],
                  name="moe_router")
    expert, weight = _moe_route(logits)
    slot_dst, block_expert, n_used = _moe_dispatch(expert)
    y2 = _moe_experts(h2, slot_dst, block_expert, n_used,
                      lw["moe_w_gate"], lw["moe_w_up"], lw["moe_w_down"])

    tmc = min(256, tile)
    in_specs = [pl.BlockSpec((tmc, d), lambda i: (i, 0)),
                pl.BlockSpec((tmc, d), lambda i: (i, 0)),
                pl.BlockSpec((tmc, d), lambda i: (rows // tmc + i, 0)),
                pl.BlockSpec((tmc, MOE_TOP_K), lambda i: (i, 0)),
                pl.BlockSpec((1, 1, d), mod_blk(5, tmc, row0))]
    args = [x1, y2, y2, weight, modv]
    if final_g is None:
        body, name = _combine_body, "moe_combine"
    else:
        body, name = _final_norm_body, "moe_combine_final_norm"
        in_specs.append(pl.BlockSpec((1, d), lambda i: (0, 0)))
        args.append(final_g.reshape(1, d))
    return pl.pallas_call(
        body,
        grid=(rows // tmc,),
        in_specs=in_specs,
        out_specs=pl.BlockSpec((tmc, d), lambda i: (i, 0)),
        out_shape=jax.ShapeDtypeStruct((rows, d), f32),
        compiler_params=_params(1),
        name=name,
    )(*args)


def _layer_weights(layer, d, w_in, mla_q_norm_g, mla_kv_norm_g, mla_w_uq, mla_w_ukv, gla_w_a2, gla_b_a,
                   gla_onorm_g, hgrn_lb, hgrn_onorm_g, w_branch, w_out, norm1_g, norm2_g,
                   moe_w_group, moe_b_group, moe_w_expert, moe_b_expert, moe_w_gate, moe_w_up, moe_w_down):
    widths = (MLA_Q_RANK, MLA_KV_RANK, MLA_ROPE,
              GLA_HEADS * GLA_DK, GLA_HEADS * GLA_DK, GLA_HEADS * GLA_DV, GLA_HEADS * GLA_DV,
              GLA_GATE_RANK, GLA_GATE_RANK,
              HGRN_HEADS * HGRN_DK, HGRN_HEADS * HGRN_DV, HGRN_HEADS * HGRN_DK, HGRN_HEADS * HGRN_DK,
              HGRN_HEADS * HGRN_DV, d, d, d)
    offs = [0]
    for w in widths:
        offs.append(offs[-1] + w)
    assert offs[-1] == w_in.shape[2]
    col = lambda a, e: lax.slice(w_in, (layer, 0, offs[a]), (layer + 1, d, offs[e])).reshape(d, offs[e] - offs[a])
    w_kr = col(2, 3)
    w_small = jnp.concatenate(
        [w_kr, _rope_swap_cols(w_kr), col(7, 9),
         jnp.zeros((d, 2 * LANES - 2 * MLA_ROPE - 2 * GLA_GATE_RANK), f32)], axis=-1)
    w_big = jnp.concatenate([col(3, 7), col(9, 11), col(13, 14), col(14, 17)], axis=-1)
    uq = mla_w_uq[layer]
    uq_rope = uq[..., MLA_NOPE:]
    w_uq = jnp.concatenate([uq[..., :MLA_NOPE], uq_rope, _rope_swap_cols(uq_rope)], axis=-1)
    wa = jnp.zeros((2, LANES, GLA_HEADS * GLA_DK), f32)
    wa = wa.at[0, :GLA_GATE_RANK].set(gla_w_a2[layer, 0])
    wa = wa.at[1, GLA_GATE_RANK:2 * GLA_GATE_RANK].set(gla_w_a2[layer, 1])
    wa = wa.reshape(2, LANES, GLA_HEADS, GLA_DK).transpose(2, 0, 1, 3)
    ba = gla_b_a[layer].reshape(2, GLA_HEADS, 1, GLA_DK).transpose(1, 0, 2, 3)
    lb = hgrn_lb[layer].reshape(HGRN_HEADS, 1, HGRN_DK)
    w_router = jnp.concatenate(
        [moe_w_group[layer], moe_w_expert[layer], jnp.zeros((d, LANES - MOE_GROUPS - MOE_EXPERTS), f32)], axis=-1)
    b_router = jnp.concatenate(
        [moe_b_group[layer], moe_b_expert[layer], jnp.zeros((LANES - MOE_GROUPS - MOE_EXPERTS,), f32)])
    return {
        "norm1_g": norm1_g[layer], "norm2_g": norm2_g[layer],
        "w_cq": col(0, 1).astype(bf16), "w_ckv": col(1, 2).astype(bf16),
        "w_small": w_small.astype(bf16), "w_big": w_big.astype(bf16), "w_hf": col(11, 13).astype(bf16),
        "q_norm_g": mla_q_norm_g[layer].reshape(1, -1), "kv_norm_g": mla_kv_norm_g[layer].reshape(1, -1),
        "w_uq": w_uq.reshape(MLA_Q_RANK, MLA_HEADS * MLA_QK).astype(bf16),
        "w_ukv": mla_w_ukv[layer].reshape(MLA_KV_RANK, MLA_HEADS * (MLA_NOPE + MLA_V)).astype(bf16),
        "gla_wa": wa.astype(bf16), "gla_ba": ba, "gla_onorm_g": gla_onorm_g[layer].reshape(1, -1),
        "hgrn_log_lb": jnp.log(lb) * LOG2E, "hgrn_log_1mlb": jnp.log1p(-lb) * LOG2E, "hgrn_1mlb": 1.0 - lb,
        "hgrn_onorm_g": hgrn_onorm_g[layer].reshape(1, -1),
        "w_branch": w_branch[layer].astype(bf16), "w_out": w_out[layer].astype(bf16),
        "w_router": w_router.astype(bf16), "b_router": b_router.reshape(1, LANES),
        "moe_w_gate": moe_w_gate[layer].astype(bf16), "moe_w_up": moe_w_up[layer].astype(bf16),
        "moe_w_down": moe_w_down[layer].astype(bf16),
    }


def kernel(x, c, ctx, c_ctx, w_mod, b_mod, norm1_g, w_in, mla_q_norm_g, mla_kv_norm_g, mla_w_uq, mla_w_ukv, gla_w_a2, gla_b_a, gla_onorm_g, hgrn_lb_logits, hgrn_onorm_g, w_branch, w_out, norm2_g, moe_w_group, moe_b_group, moe_w_expert, moe_b_expert, moe_w_gate, moe_w_up, moe_w_down, final_norm_g):
    b, s, d = x.shape
    n_ctx = ctx.shape[1]
    depth = w_mod.shape[0]
    nc, nl = b * n_ctx, b * s
    assert nc % s == 0 and s % GRID_W == 0 and n_ctx % SCAN_CHUNK == 0 and s % SCAN_CHUNK == 0
    tile = 1
    while nc % (tile * 2) == 0 and s % (tile * 2) == 0:
        tile *= 2
    dims = {"B": b, "S": s, "n_ctx": n_ctx, "D": d, "Nc": nc, "Nl": nl, "N": nc + nl, "tile": tile}

    n_cond = -(-(b + 1) // 8) * 8
    cond = jnp.concatenate([c, c_ctx[None, :], jnp.zeros((n_cond - b - 1, d), f32)], axis=0)
    mod_all = _mod_vectors(cond, w_mod, b_mod)

    lb_cum = jnp.cumsum(jax.nn.softmax(hgrn_lb_logits.astype(f32), axis=0), axis=0)
    hgrn_lb = lb_cum - lb_cum[:1]

    xg = jnp.concatenate([ctx.reshape(nc, d), x.reshape(nl, d)], axis=0)
    for layer in range(depth):
        with_ctx = layer < depth - 1
        lw = _layer_weights(layer, d, w_in, mla_q_norm_g, mla_kv_norm_g, mla_w_uq, mla_w_ukv, gla_w_a2,
                            gla_b_a, gla_onorm_g, hgrn_lb, hgrn_onorm_g, w_branch, w_out, norm1_g, norm2_g,
                            moe_w_group, moe_b_group, moe_w_expert, moe_b_expert, moe_w_gate, moe_w_up,
                            moe_w_down)
        modv = mod_all[layer, :b + 1].reshape((b + 1) * 6, 1, d)
        xg = _layer(xg, modv, lw, dims, with_ctx, None if with_ctx else final_norm_g)
    return xg.reshape(b, s, d)
```

```python
import functools

import jax
import jax.numpy as jnp
from jax import lax
from jax.experimental import pallas as pl
from jax.experimental.pallas import tpu as pltpu

f32 = jnp.float32
bf16 = jnp.bfloat16

EPS = 1e-6
LOG2E = 1.4426950408889634
ROPE_THETA = 10000.0
GRID_W = 64

MLA_HEADS = 16
MLA_Q_RANK = 1024
MLA_KV_RANK = 512
MLA_NOPE = 128
MLA_ROPE = 64
MLA_V = 128
MLA_QK = 256
GLA_HEADS = 8
GLA_DK = 128
GLA_DV = 256
GLA_GATE_RANK = 16
GLA_TAU = 16.0
HGRN_HEADS = 16
HGRN_DK = 128
HGRN_DV = 128
BRANCH_WIDTH = 2048
MOE_GROUPS = 4
MOE_EPG = 4
MOE_EXPERTS = 16
MOE_TOP_K = 2
MOE_BLOCK = 256

SCAN_CHUNK = 64
LANES = 128

VMEM_LIMIT_BYTES = 56 * 1024 * 1024


def _params(n_axes):
    return pltpu.CompilerParams(dimension_semantics=("arbitrary",) * n_axes,
                                vmem_limit_bytes=VMEM_LIMIT_BYTES)


def _dot(a, b):
    return jnp.dot(a, b, preferred_element_type=f32)


def _dot_nt(a, b):
    return lax.dot_general(a, b, (((1,), (1,)), ((), ())), preferred_element_type=f32)


def _dot_tn(a, b):
    return lax.dot_general(a, b, (((0,), (0,)), ((), ())), preferred_element_type=f32)


def _sigmoid(x):
    return 1.0 / (1.0 + jnp.exp(-x))


def _mm_body(x_ref, w_ref, *rest, n_extra, epilogue, w_is_t):
    x, w = x_ref[...].astype(bf16), w_ref[...].astype(bf16)
    epilogue(_dot_nt(x, w) if w_is_t else _dot(x, w), rest[:n_extra], rest[n_extra:])


def _mm(x, w, epilogue, *, tm, tn, rows, ncols, outs, extras=(), x_blk0=0, w_is_t=False, w_blk0=0, name):
    k = x.shape[1]
    assert rows % tm == 0 and ncols % tn == 0 and w.shape[1 if w_is_t else 0] % k == 0
    w_spec = (pl.BlockSpec((tn, k), lambda i, j: (j, 0)) if w_is_t
              else pl.BlockSpec((k, tn), lambda i, j: (w_blk0, j)))
    in_specs = [pl.BlockSpec((tm, k), lambda i, j: (i + x_blk0, 0)), w_spec]
    in_specs += [pl.BlockSpec(bs, im) for _, bs, im in extras]
    res = pl.pallas_call(
        functools.partial(_mm_body, n_extra=len(extras), epilogue=epilogue, w_is_t=w_is_t),
        grid=(rows // tm, ncols // tn),
        in_specs=in_specs,
        out_specs=[pl.BlockSpec(bs, im) for _, bs, im in outs],
        out_shape=[s for s, _, _ in outs],
        compiler_params=_params(2),
        name=name,
    )(x, w, *[a for a, _, _ in extras])
    return res


def _ep_cast(acc, extras, outs):
    outs[0][...] = acc.astype(outs[0].dtype)


def _ep_bias(acc, extras, outs):
    outs[0][...] = (acc + extras[0][...]).astype(outs[0].dtype)


def _ep_rmsnorm(acc, extras, outs):
    ms = jnp.mean(acc * acc, axis=-1, keepdims=True)
    outs[0][...] = (acc * lax.rsqrt(ms + EPS) * extras[0][...]).astype(outs[0].dtype)


def _rotate(blk, cos_t, sin_t):
    return blk * cos_t + pltpu.roll(blk, 64, axis=1) * sin_t


def _ep_rope_small(acc, extras, outs):
    cos_t, sin_t = extras[0][...], extras[1][...]
    outs[0][...] = _rotate(acc[:, :LANES], cos_t, sin_t).astype(outs[0].dtype)
    outs[1][...] = acc[:, LANES:]


def _ep_rope_q(acc, extras, outs):
    cos_t, sin_t = extras[0][...], extras[1][...]
    o = outs[0]
    for hh in range(acc.shape[1] // MLA_QK):
        c0 = hh * MLA_QK
        o[:, c0:c0 + LANES] = acc[:, c0:c0 + LANES].astype(o.dtype)
        o[:, c0 + LANES:c0 + MLA_QK] = _rotate(acc[:, c0 + LANES:c0 + MLA_QK], cos_t, sin_t).astype(o.dtype)


def _ep_residual(acc, extras, outs):
    outs[0][...] = extras[0][...] + extras[1][0] * acc


def _mod_body(c_ref, w_ref, b_ref, o_ref):
    c = c_ref[...]
    a = (c * _sigmoid(c)).astype(bf16)
    o_ref[0] = _dot(a, w_ref[0].astype(bf16)) + b_ref[0]


def _mod_vectors(cond, w_mod, b_mod):
    n_l, d, d6 = w_mod.shape
    r = cond.shape[0]
    tn = min(512, d6)
    return pl.pallas_call(
        _mod_body,
        grid=(n_l, d6 // tn),
        in_specs=[pl.BlockSpec((r, d), lambda l, j: (0, 0)),
                  pl.BlockSpec((1, d, tn), lambda l, j: (l, 0, j)),
                  pl.BlockSpec((1, 1, tn), lambda l, j: (l, 0, j))],
        out_specs=pl.BlockSpec((1, r, tn), lambda l, j: (l, 0, j)),
        out_shape=jax.ShapeDtypeStruct((n_l, r, d6), f32),
        compiler_params=_params(2),
        name="mod_vectors",
    )(cond, w_mod, b_mod.reshape(n_l, 1, d6))


def _norm_mod_body(x_ref, g_ref, shift_ref, scale_ref, o_ref):
    x = x_ref[...]
    ms = jnp.mean(x * x, axis=-1, keepdims=True)
    y = x * lax.rsqrt(ms + EPS) * g_ref[...]
    o_ref[...] = (y * (1.0 + scale_ref[0]) + shift_ref[0]).astype(o_ref.dtype)


def _moe_sum(x_ref, y0_ref, y1_ref, w_ref, gate_ref):
    w = w_ref[...]
    return x_ref[...] + gate_ref[0] * (y0_ref[...] * w[:, 0:1] + y1_ref[...] * w[:, 1:2])


def _final_norm_body(x_ref, y0_ref, y1_ref, w_ref, gate_ref, g_ref, o_ref):
    x = _moe_sum(x_ref, y0_ref, y1_ref, w_ref, gate_ref)
    ms = jnp.mean(x * x, axis=-1, keepdims=True)
    o_ref[...] = x * lax.rsqrt(ms + EPS) * g_ref[...]


def _combine_body(x_ref, y0_ref, y1_ref, w_ref, gate_ref, o_ref):
    o_ref[...] = _moe_sum(x_ref, y0_ref, y1_ref, w_ref, gate_ref)


def _attn_lat_body(q_ref, knc_ref, knl_ref, vc_ref, vl_ref, krc_ref, krl_ref, o_ref,
                   k_scr, v_scr, *, n_ctx, tq, scale):
    k_scr[:n_ctx, :LANES] = knc_ref[...]
    k_scr[:n_ctx, LANES:] = krc_ref[...]
    k_scr[n_ctx:, :LANES] = knl_ref[...]
    k_scr[n_ctx:, LANES:] = krl_ref[...]
    v_scr[:n_ctx, :] = vc_ref[...]
    v_scr[n_ctx:, :] = vl_ref[...]

    c2 = scale * LOG2E

    def body(t, carry):
        r = pl.multiple_of(t * tq, tq)
        q = q_ref[pl.ds(r, tq), :]
        s = _dot_nt(q, k_scr[...])
        m = jnp.max(s, axis=-1, keepdims=True)
        p = jnp.exp2(s * c2 - m * c2)
        l = jnp.sum(p, axis=-1, keepdims=True)
        o = _dot(p.astype(bf16), v_scr[...])
        o_ref[pl.ds(r, tq), :] = (o / l).astype(o_ref.dtype)
        return carry

    lax.fori_loop(0, q_ref.shape[0] // tq, body, 0)


def _attn_ctx_body(q_ref, kn_ref, v_ref, kr_ref, o_ref, *, scale):
    k = jnp.concatenate([kn_ref[...], kr_ref[...]], axis=1)
    s = _dot_nt(q_ref[...], k) * scale
    m = jnp.max(s, axis=-1, keepdims=True)
    p = jnp.exp(s - m)
    l = jnp.sum(p, axis=-1, keepdims=True)
    o_ref[...] = (_dot(p.astype(bf16), v_ref[...]) / l).astype(o_ref.dtype)


def _attention(q, kv, kr, dims, with_ctx):
    b, s, n_ctx = dims["B"], dims["S"], dims["n_ctx"]
    nc, nl = dims["Nc"], dims["Nl"]
    lb0 = nc // s
    scale = (MLA_NOPE + MLA_ROPE) ** -0.5
    tq = min(256, s)
    o_lat = pl.pallas_call(
        functools.partial(_attn_lat_body, n_ctx=n_ctx, tq=tq, scale=scale),
        grid=(b, MLA_HEADS),
        in_specs=[
            pl.BlockSpec((s, MLA_QK), lambda i, h: (lb0 + i, h)),
            pl.BlockSpec((n_ctx, LANES), lambda i, h: (i, 2 * h)),
            pl.BlockSpec((s, LANES), lambda i, h: (lb0 + i, 2 * h)),
            pl.BlockSpec((n_ctx, LANES), lambda i, h: (i, 2 * h + 1)),
            pl.BlockSpec((s, LANES), lambda i, h: (lb0 + i, 2 * h + 1)),
            pl.BlockSpec((n_ctx, LANES), lambda i, h: (i, 0)),
            pl.BlockSpec((s, LANES), lambda i, h: (lb0 + i, 0)),
        ],
        out_specs=pl.BlockSpec((s, MLA_V), lambda i, h: (i, h)),
        out_shape=jax.ShapeDtypeStruct((nl, MLA_HEADS * MLA_V), bf16),
        scratch_shapes=[pltpu.VMEM((n_ctx + s, MLA_QK), bf16), pltpu.VMEM((n_ctx + s, MLA_V), bf16)],
        compiler_params=_params(2),
        name="mla_attention_latent",
    )(q, kv, kv, kv, kv, kr, kr)
    if not with_ctx:
        return None, o_lat
    o_ctx = pl.pallas_call(
        functools.partial(_attn_ctx_body, scale=scale),
        grid=(b, MLA_HEADS),
        in_specs=[
            pl.BlockSpec((n_ctx, MLA_QK), lambda i, h: (i, h)),
            pl.BlockSpec((n_ctx, LANES), lambda i, h: (i, 2 * h)),
            pl.BlockSpec((n_ctx, LANES), lambda i, h: (i, 2 * h + 1)),
            pl.BlockSpec((n_ctx, LANES), lambda i, h: (i, 0)),
        ],
        out_specs=pl.BlockSpec((n_ctx, MLA_V), lambda i, h: (i, h)),
        out_shape=jax.ShapeDtypeStruct((nc, MLA_HEADS * MLA_V), bf16),
        compiler_params=_params(2),
        name="mla_attention_context",
    )(q, kv, kv, kr)
    return o_ctx, o_lat


def _scan_levels(reverse):
    c = SCAN_CHUNK
    row = lax.broadcasted_iota(jnp.int32, (c, c), 0)
    col = lax.broadcasted_iota(jnp.int32, (c, c), 1)
    earlier = (row < col) if reverse else (row > col)
    x = row ^ col
    x = x | (x >> 1)
    x = x | (x >> 2)
    x = x | (x >> 4)
    top_bit = (x + 1) >> 1
    return jnp.where(earlier, top_bit, jnp.where(row == col, c, 0))


def _scan_chunks(items):
    c = SCAN_CHUNK
    kw = items[0][3].shape[1]
    row_k = lax.broadcasted_iota(jnp.int32, (c, kw), 0)
    sub8 = lax.broadcasted_iota(jnp.int32, (8, kw), 0)

    cums = []
    for q, k, v_bf, lg2, st, lvl, reverse in items:
        cum = lg2
        for s in (1, 2, 4):
            if reverse:
                cum = cum + jnp.where(row_k < c - s, pltpu.roll(cum, c - s, axis=0), 0.0)
            else:
                cum = cum + jnp.where(row_k >= s, pltpu.roll(cum, s, axis=0), 0.0)
        for s in (8, 16, 32):
            if reverse:
                cum = jnp.concatenate([cum[:c - s] + cum[s:], cum[c - s:]], axis=0)
            else:
                cum = jnp.concatenate([cum[:s], cum[s:] + cum[:c - s]], axis=0)
        cums.append(cum)

    operands = []
    for (q, k, v_bf, lg2, st, lvl, reverse), cum in zip(items, cums):
        tot = cum[0:1] if reverse else cum[c - 1:c]
        qe = (q * jnp.exp2(cum)).astype(bf16)
        kd = (k * jnp.exp2(tot - cum)).astype(bf16)
        levels = []
        h = c // 2
        while h >= 1:
            later = ((row_k & h) == 0) if reverse else ((row_k & h) != 0)
            if h >= 8:
                groups = []
                for r0 in range(0, c, 8):
                    blk = r0 // (2 * h) * (2 * h)
                    bnd = cum[blk + h - (0 if reverse else 1):blk + h + (1 if reverse else 0)]
                    if ((r0 - blk) >= h) != reverse:
                        groups.append(q[r0:r0 + 8] * jnp.exp2(cum[r0:r0 + 8] - bnd))
                    else:
                        groups.append(k[r0:r0 + 8] * jnp.exp2(bnd - cum[r0:r0 + 8]))
                m = jnp.concatenate(groups, axis=0)
            elif h > 1:
                groups = []
                for r0 in range(0, c, 8):
                    bnds = [jnp.broadcast_to(cum[b + h - (0 if reverse else 1):b + h + (1 if reverse else 0)],
                                             (8, kw)) for b in range(r0, r0 + 8, 2 * h)]
                    groups.append(bnds[0] if len(bnds) == 1 else jnp.where(sub8 < 4, bnds[0], bnds[1]))
                d = cum - jnp.concatenate(groups, axis=0)
                m = jnp.where(later, q, k) * jnp.exp2(-jnp.abs(d))
            else:
                m = jnp.where(later, q * jnp.exp2(lg2), k)
            levels.append((h, m.astype(bf16)))
            h //= 2
        operands.append((tot, qe, kd, st.astype(bf16), levels))

    products = []
    for (q, k, v_bf, lg2, st, lvl, reverse), (tot, qe, kd, st_bf, levels) in zip(items, operands):
        products.append((_dot_nt(qe, st_bf), _dot_tn(v_bf, kd), [(h, _dot_nt(m, m)) for h, m in levels]))

    scores = []
    for (q, k, v_bf, lg2, st, lvl, reverse), (o_inter, upd, ps) in zip(items, products):
        a = jnp.where(lvl == c, jnp.sum(q * k, axis=-1, keepdims=True), 0.0)
        for h, p in ps:
            a = jnp.where(lvl == h, p, a)
        scores.append(a.astype(bf16))

    results = []
    for (q, k, v_bf, lg2, st, lvl, reverse), (tot, _, _, _, _), (o_inter, upd, _), a in zip(
            items, operands, products, scores):
        results.append((o_inter + _dot(a, v_bf), st * jnp.exp2(tot) + upd))
    return results


def _rms_rows(x, g):
    ms = jnp.mean(x * x, axis=-1, keepdims=True)
    return x * lax.rsqrt(ms + EPS) * g


def _scan_body(*refs, mode, n_ctx, seq):
    if mode == "gla":
        (q_c, q_l, k_c, k_l, v_c, v_l, r_c, r_l, ga_c, ga_l, wa_ref, ba_ref, g_ref,
         o_c, o_l, sf_ref, sb_ref, of_ref, ob_ref, lvl_ref) = refs
    else:
        (q_c, q_l, v_c, v_l, r_c, r_l, ff_c, ff_l, fb_c, fb_l, la_ref, lc_ref, oml_ref, g_ref,
         o_c, o_l, sf_ref, sb_ref, of_ref, ob_ref, lvl_ref) = refs
    c = SCAN_CHUNK
    st_dir = (sf_ref, sb_ref)
    o_dir = (of_ref, ob_ref)
    for d in range(2):
        st_dir[d][...] = jnp.zeros_like(st_dir[d])
        lvl_ref[d] = _scan_levels(bool(d))

    def chunk_inputs(seg, r, d):
        q = (q_l if seg else q_c)[pl.ds(r, c), :].astype(f32)
        v = (v_l if seg else v_c)[pl.ds(r, c), :]
        if mode == "gla":
            k = (k_l if seg else k_c)[pl.ds(r, c), :].astype(f32)
            ga = (ga_l if seg else ga_c)[pl.ds(r, c), :].astype(bf16)
            x2 = (_dot(ga, wa_ref[0, d]) + ba_ref[0, d]) * LOG2E
            lg2 = (jnp.minimum(x2, 0.0) - jnp.log2(1.0 + jnp.exp2(-jnp.abs(x2)))) * (1.0 / GLA_TAU)
            q = q * (GLA_DK ** -0.5)
        else:
            fref = ((ff_l if seg else ff_c), (fb_l if seg else fb_c))[d]
            f2 = fref[pl.ds(r, c), :] * LOG2E
            ls2 = jnp.minimum(f2, 0.0) - jnp.log2(1.0 + jnp.exp2(-jnp.abs(f2)))
            la2 = la_ref[0]
            bb2 = lc_ref[0] + ls2
            lg2 = jnp.maximum(la2, bb2) + jnp.log2(1.0 + jnp.exp2(-jnp.abs(la2 - bb2)))
            k = oml_ref[0] * jnp.exp2(ls2 - f2)
        return q, k, v, lg2

    for seg, (n_rows, base) in enumerate(((n_ctx, 0), (seq, n_ctx))):
        n_chunks = n_rows // c

        def body(i, carry, seg=seg, n_chunks=n_chunks, base=base):
            rows, items = [], []
            for d in range(2):
                ci = (n_chunks - 1 - i) if d else i
                r = pl.multiple_of(ci * c, c)
                rows.append(r)
                items.append(chunk_inputs(seg, r, d) + (st_dir[d][...], lvl_ref[d], bool(d)))
            for d, (o, st_new) in enumerate(_scan_chunks(items)):
                st_dir[d][...] = st_new
                o_dir[d][pl.ds(base + rows[d], c), :] = o
            return carry

        lax.fori_loop(0, n_chunks, body, 0, unroll=8 if n_chunks % 8 == 0 else 4)

    tr = min(256, n_ctx, seq)
    g = g_ref[...]
    for seg, (n_rows, base) in enumerate(((n_ctx, 0), (seq, n_ctx))):
        r_ref = r_l if seg else r_c
        o_ref = o_l if seg else o_c

        def post(t, carry, base=base, r_ref=r_ref, o_ref=o_ref):
            r = pl.multiple_of(t * tr, tr)
            o = of_ref[pl.ds(base + r, tr), :] + ob_ref[pl.ds(base + r, tr), :]
            gate = r_ref[pl.ds(r, tr), :].astype(f32)
            if mode == "gla":
                res = gate * _sigmoid(gate) * _rms_rows(o, g)
            else:
                res = _rms_rows(o * _sigmoid(gate), g)
            o_ref[pl.ds(r, tr), :] = res.astype(o_ref.dtype)
            return carry

        lax.fori_loop(0, n_rows // tr, post, 0)


def _scan_call(mode, tok_inputs, const_inputs, dims, n_heads, dv):
    b, s, n_ctx = dims["B"], dims["S"], dims["n_ctx"]
    nc, nl = dims["Nc"], dims["Nl"]
    lb0 = nc // s
    in_specs, args = [], []
    for arr, width, colf in tok_inputs:
        in_specs.append(pl.BlockSpec((n_ctx, width), lambda i, h, colf=colf: (i, colf(h))))
        in_specs.append(pl.BlockSpec((s, width), lambda i, h, colf=colf: (lb0 + i, colf(h))))
        args += [arr, arr]
    for arr, bs, im in const_inputs:
        in_specs.append(pl.BlockSpec(bs, im))
        args.append(arr)
    return pl.pallas_call(
        functools.partial(_scan_body, mode=mode, n_ctx=n_ctx, seq=s),
        grid=(b, n_heads),
        in_specs=in_specs,
        out_specs=[pl.BlockSpec((n_ctx, dv), lambda i, h: (i, h)),
                   pl.BlockSpec((s, dv), lambda i, h: (i, h))],
        out_shape=[jax.ShapeDtypeStruct((nc, n_heads * dv), bf16),
                   jax.ShapeDtypeStruct((nl, n_heads * dv), bf16)],
        scratch_shapes=[pltpu.VMEM((dv, LANES), f32),
                        pltpu.VMEM((dv, LANES), f32),
                        pltpu.VMEM((n_ctx + s, dv), f32),
                        pltpu.VMEM((n_ctx + s, dv), f32),
                        pltpu.VMEM((2, SCAN_CHUNK, SCAN_CHUNK), jnp.int32)],
        compiler_params=_params(2),
        name="scan_" + mode,
    )(*args)


def _merge_body(oa_ref, ob_ref, oc_ref, w_ref, za_ref, zb_ref, zc_ref, m_ref):
    acc = _sigmoid(za_ref[...].astype(f32)) * _dot(oa_ref[...], w_ref[0])
    acc += _sigmoid(zb_ref[...].astype(f32)) * _dot(ob_ref[...], w_ref[1])
    acc += _sigmoid(zc_ref[...].astype(f32)) * _dot(oc_ref[...], w_ref[2])
    m_ref[...] = acc.astype(m_ref.dtype)


def _merge(o_a, o_b, o_c, w_branch, layer, big, z_col0, rows, row0, tm, d):
    tn = min(512, d)
    assert rows % tm == 0 and row0 % tm == 0 and z_col0 % tn == 0
    zb0 = z_col0 // tn
    nzb = d // tn
    rb0 = row0 // tm
    o_spec = pl.BlockSpec((tm, BRANCH_WIDTH), lambda i, j: (i, 0))
    return pl.pallas_call(
        _merge_body,
        grid=(rows // tm, d // tn),
        in_specs=[o_spec, o_spec, o_spec,
                  pl.BlockSpec((3, BRANCH_WIDTH, tn), lambda i, j: (layer, 0, j)),
                  pl.BlockSpec((tm, tn), lambda i, j: (rb0 + i, zb0 + j)),
                  pl.BlockSpec((tm, tn), lambda i, j: (rb0 + i, zb0 + nzb + j)),
                  pl.BlockSpec((tm, tn), lambda i, j: (rb0 + i, zb0 + 2 * nzb + j))],
        out_specs=pl.BlockSpec((tm, tn), lambda i, j: (i, j)),
        out_shape=jax.ShapeDtypeStruct((rows, d), bf16),
        compiler_params=_params(2),
        name="branch_merge",
    )(o_a, o_b, o_c, w_branch, big, big, big)


def _moe_body(be_ref, nused_ref, dst_ref, h_hbm, wg_ref, wu_ref, wd_ref, y_hbm,
              xbuf, ybuf, sem_in, sem_out, *, rows):
    i = pl.program_id(0)
    blk = MOE_BLOCK

    @pl.when(i == 0)
    def _():
        ybuf[...] = jnp.zeros_like(ybuf)
        spare = pltpu.make_async_copy(ybuf, y_hbm.at[pl.ds(MOE_TOP_K * rows, blk)], sem_out)
        spare.start()
        spare.wait()

    n_used = nused_ref[0]
    slot = i % 2

    def gather(step, buf):
        def issue(r, carry):
            tok = jnp.maximum(dst_ref[step * blk + r], 0) >> 1
            pltpu.make_async_copy(h_hbm.at[pl.ds(tok, 1)], xbuf.at[buf, pl.ds(r, 1)], sem_in.at[buf]).start()
            return carry

        lax.fori_loop(0, blk, issue, 0, unroll=8)

    def wait_scatter():
        pltpu.make_async_copy(ybuf, y_hbm.at[pl.ds(0, blk)], sem_out).wait()

    @pl.when(i == 0)
    def _():
        gather(0, 0)

    @pl.when(i + 1 < n_used)
    def _():
        gather(i + 1, 1 - slot)

    @pl.when(i < n_used)
    def _():
        pltpu.make_async_copy(h_hbm.at[pl.ds(0, blk)], xbuf.at[slot], sem_in.at[slot]).wait()
        x = xbuf[slot].astype(bf16)
        gate = _dot(x, wg_ref[0])
        up = _dot(x, wu_ref[0])
        act = (gate * _sigmoid(gate) * up).astype(bf16)
        y = _dot(act, wd_ref[0])

        @pl.when(i > 0)
        def _():
            wait_scatter()

        ybuf[...] = y

        def put(r, carry):
            a = dst_ref[i * blk + r]
            row = jnp.where(a >= 0, (a & 1) * rows + (a >> 1), MOE_TOP_K * rows + r)
            pltpu.make_async_copy(ybuf.at[pl.ds(r, 1)], y_hbm.at[pl.ds(row, 1)], sem_out).start()
            return carry

        lax.fori_loop(0, blk, put, 0, unroll=8)

        @pl.when(i == n_used - 1)
        def _():
            wait_scatter()


def _moe_experts(h2, slot_dst, block_expert, n_used, wg, wu, wd):
    rows, d = h2.shape
    n_blocks = block_expert.shape[0]
    hid = wg.shape[-1]
    assert MOE_TOP_K == 2
    grid_spec = pltpu.PrefetchScalarGridSpec(
        num_scalar_prefetch=3,
        grid=(n_blocks,),
        in_specs=[
            pl.BlockSpec(memory_space=pl.ANY),
            pl.BlockSpec((1, d, hid), lambda i, be, nu, ds: (be[i], 0, 0)),
            pl.BlockSpec((1, d, hid), lambda i, be, nu, ds: (be[i], 0, 0)),
            pl.BlockSpec((1, hid, d), lambda i, be, nu, ds: (be[i], 0, 0)),
        ],
        out_specs=pl.BlockSpec(memory_space=pl.ANY),
        scratch_shapes=[pltpu.VMEM((2, MOE_BLOCK, d), f32), pltpu.VMEM((MOE_BLOCK, d), f32),
                        pltpu.SemaphoreType.DMA((2,)), pltpu.SemaphoreType.DMA(())],
    )
    return pl.pallas_call(
        functools.partial(_moe_body, rows=rows),
        grid_spec=grid_spec,
        out_shape=jax.ShapeDtypeStruct((rows * MOE_TOP_K + MOE_BLOCK, d), f32),
        compiler_params=_params(1),
        name="moe_experts",
    )(block_expert, n_used, slot_dst, h2, wg, wu, wd)


def _argmax_cols(cols):
    idx = jnp.zeros(cols[0].shape, jnp.int32)
    best = cols[0]
    for j in range(1, len(cols)):
        better = cols[j] > best
        idx = jnp.where(better, j, idx)
        best = jnp.where(better, cols[j], best)
    return idx, best


def _moe_route(logits):
    g = [logits[:, j] for j in range(MOE_GROUPS)]
    g_idx, g_max = _argmax_cols(g)
    g_prob = 1.0 / sum(jnp.exp(gj - g_max) for gj in g)
    e = []
    for j in range(MOE_EPG):
        ej = logits[:, MOE_GROUPS + j]
        for grp in range(1, MOE_GROUPS):
            ej = jnp.where(g_idx == grp, logits[:, MOE_GROUPS + grp * MOE_EPG + j], ej)
        e.append(ej)
    i1, v1 = _argmax_cols(e)
    i2, v2 = _argmax_cols([jnp.where(i1 == j, -jnp.inf, e[j]) for j in range(MOE_EPG)])
    t = jnp.exp(v2 - v1)
    weight = jnp.stack([1.0 / (1.0 + t), t / (1.0 + t)], axis=-1) * g_prob[:, None]
    expert = g_idx[:, None] * MOE_EPG + jnp.stack([i1, i2], axis=-1)
    return expert, weight


def _moe_dispatch(expert):
    expert = expert.reshape(-1)
    n_assign = expert.shape[0]
    ids = jnp.arange(MOE_EXPERTS, dtype=jnp.int32)
    onehot = (expert[:, None] == ids[None, :]).astype(jnp.int32)
    csum = jnp.cumsum(onehot, axis=0)
    rank = jnp.sum(onehot * csum, axis=1) - 1
    counts = csum[-1]
    padded = (counts + MOE_BLOCK - 1) // MOE_BLOCK * MOE_BLOCK
    pad_end = jnp.cumsum(padded)
    dest = jnp.sum(onehot * (pad_end - padded)[None, :], axis=1) + rank
    n_blocks = -(-n_assign // MOE_BLOCK) + MOE_EXPERTS
    slot_dst = jnp.full((n_blocks * MOE_BLOCK,), -1, jnp.int32).at[dest].set(
        jnp.arange(n_assign, dtype=jnp.int32))
    starts = jnp.arange(n_blocks, dtype=jnp.int32) * MOE_BLOCK
    block_expert = jnp.minimum(jnp.sum((pad_end[None, :] <= starts[:, None]).astype(jnp.int32), axis=1),
                               MOE_EXPERTS - 1)
    n_used = pad_end[-1:] // MOE_BLOCK
    return slot_dst, block_expert.astype(jnp.int32), n_used.astype(jnp.int32)


def _rope_swap_cols(w):
    qd = MLA_ROPE // 4
    return jnp.concatenate([-w[..., qd:2 * qd], w[..., :qd], -w[..., 3 * qd:], w[..., 2 * qd:3 * qd]], axis=-1)


def _rope_tables(seq, tm):
    qd = MLA_ROPE // 4
    n_rows = seq // GRID_W
    inv_freq = ROPE_THETA ** (-jnp.arange(qd, dtype=f32) / qd)
    row = jnp.repeat(jnp.arange(n_rows, dtype=f32), GRID_W)
    col = jnp.tile(jnp.arange(GRID_W, dtype=f32), n_rows)
    ar, ac = row[:, None] * inv_freq, col[:, None] * inv_freq
    zeros = jnp.zeros((seq, LANES - MLA_ROPE), f32)
    cos_t = jnp.concatenate([jnp.cos(ar), jnp.cos(ar), jnp.cos(ac), jnp.cos(ac), zeros], axis=-1)
    sin_t = jnp.concatenate([jnp.sin(ar), jnp.sin(ar), jnp.sin(ac), jnp.sin(ac), zeros], axis=-1)
    ident = jnp.concatenate([jnp.ones((tm, MLA_ROPE), f32), jnp.zeros((tm, LANES - MLA_ROPE), f32)], axis=-1)
    return (jnp.concatenate([ident, cos_t], axis=0),
            jnp.concatenate([jnp.zeros((tm, LANES), f32), sin_t], axis=0))


def _layer(xg, modv, lw, dims, with_ctx, final_g):
    b, s, n_ctx, d = dims["B"], dims["S"], dims["n_ctx"], dims["D"]
    nc, nl, n = dims["Nc"], dims["Nl"], dims["N"]
    tile = dims["tile"]

    def mod_blk(which, tm, row0):
        def im(i, *_):
            g0 = i * tm + row0
            cond = jnp.where(g0 < nc, b, jnp.maximum(g0 - nc, 0) // s)
            return (cond * 6 + which, 0, 0)
        return im

    def norm_mod(x, g, which_shift, which_scale, rows, row0, dtype):
        tm = min(256, tile)
        return pl.pallas_call(
            _norm_mod_body,
            grid=(rows // tm,),
            in_specs=[pl.BlockSpec((tm, d), lambda i: (i, 0)),
                      pl.BlockSpec((1, d), lambda i: (0, 0)),
                      pl.BlockSpec((1, 1, d), mod_blk(which_shift, tm, row0)),
                      pl.BlockSpec((1, 1, d), mod_blk(which_scale, tm, row0))],
            out_specs=pl.BlockSpec((tm, d), lambda i: (i, 0)),
            out_shape=jax.ShapeDtypeStruct((rows, d), dtype),
            compiler_params=_params(1),
            name="norm_modulate",
        )(x, g.reshape(1, d), modv, modv)

    h = norm_mod(xg, lw["norm1_g"], 0, 1, n, 0, bf16)

    tm = min(1024, tile)
    row_tile = lambda i, j: (i, 0)

    def full_out(cols, dtype, tn):
        return (jax.ShapeDtypeStruct((n, cols), dtype), (tm, tn), lambda i, j: (i, j))

    cqn, = _mm(h, lw["w_cq"], _ep_rmsnorm, tm=tm, tn=MLA_Q_RANK, rows=n, ncols=MLA_Q_RANK, w_is_t=True,
               extras=[(lw["q_norm_g"], (1, MLA_Q_RANK), lambda i, j: (0, 0))],
               outs=[full_out(MLA_Q_RANK, bf16, MLA_Q_RANK)], name="in_proj_cq")
    ckvn, = _mm(h, lw["w_ckv"], _ep_rmsnorm, tm=tm, tn=MLA_KV_RANK, rows=n, ncols=MLA_KV_RANK, w_is_t=True,
                extras=[(lw["kv_norm_g"], (1, MLA_KV_RANK), lambda i, j: (0, 0))],
                outs=[full_out(MLA_KV_RANK, bf16, MLA_KV_RANK)], name="in_proj_ckv")

    cos_t, sin_t = _rope_tables(s, tm)

    def rope_blk(i, j):
        g0 = i * tm
        return (jnp.where(g0 < nc, 0, 1 + (jnp.maximum(g0 - nc, 0) % s) // tm), 0)

    rope_extras = [(cos_t, (tm, LANES), rope_blk), (sin_t, (tm, LANES), rope_blk)]
    kr, ga = _mm(h, lw["w_small"], _ep_rope_small, tm=tm, tn=2 * LANES, rows=n, ncols=2 * LANES, w_is_t=True,
                 extras=rope_extras,
                 outs=[(jax.ShapeDtypeStruct((n, LANES), bf16), (tm, LANES), row_tile),
                       (jax.ShapeDtypeStruct((n, LANES), f32), (tm, LANES), row_tile)],
                 name="in_proj_small")
    n_big = lw["w_big"].shape[0]
    tn_big = min(1024, d)
    big, = _mm(h, lw["w_big"], _ep_cast, tm=tm, tn=tn_big, rows=n, ncols=n_big, w_is_t=True,
               outs=[full_out(n_big, bf16, tn_big)], name="in_proj_big")
    n_hf = lw["w_hf"].shape[0]
    hf, = _mm(h, lw["w_hf"], _ep_cast, tm=tm, tn=1024, rows=n, ncols=n_hf, w_is_t=True,
              outs=[full_out(n_hf, f32, 1024)], name="in_proj_hf")

    qw = MLA_HEADS * MLA_QK
    q, = _mm(cqn, lw["w_uq"], _ep_rope_q, tm=tm, tn=1024, rows=n, ncols=qw,
             extras=rope_extras, outs=[full_out(qw, bf16, 1024)], name="mla_q_up")
    kv, = _mm(ckvn, lw["w_ukv"], _ep_cast, tm=tm, tn=1024, rows=n, ncols=qw,
              outs=[full_out(qw, bf16, 1024)], name="mla_kv_up")
    o_a_ctx, o_a_lat = _attention(q, kv, kr, dims, with_ctx)

    gk0 = GLA_HEADS * GLA_DK
    gv0 = 2 * gk0
    gr0 = gv0 + GLA_HEADS * GLA_DV
    hq0 = gr0 + GLA_HEADS * GLA_DV
    hi0 = hq0 + HGRN_HEADS * HGRN_DK
    hg0 = hi0 + HGRN_HEADS * HGRN_DV
    z0 = hg0 + HGRN_HEADS * HGRN_DV
    o_b_ctx, o_b_lat = _scan_call(
        "gla",
        [(big, GLA_DK, lambda hh: hh), (big, GLA_DK, lambda hh: gk0 // GLA_DK + hh),
         (big, GLA_DV, lambda hh: gv0 // GLA_DV + hh), (big, GLA_DV, lambda hh: gr0 // GLA_DV + hh),
         (ga, LANES, lambda hh: 0)],
        [(lw["gla_wa"], (1, 2, LANES, GLA_DK), lambda i, hh: (hh, 0, 0, 0)),
         (lw["gla_ba"], (1, 2, 1, GLA_DK), lambda i, hh: (hh, 0, 0, 0)),
         (lw["gla_onorm_g"], (1, GLA_DV), lambda i, hh: (0, 0))],
        dims, GLA_HEADS, GLA_DV)

    o_c_ctx, o_c_lat = _scan_call(
        "hgrn",
        [(big, HGRN_DK, lambda hh: hq0 // HGRN_DK + hh), (big, HGRN_DV, lambda hh: hi0 // HGRN_DV + hh),
         (big, HGRN_DV, lambda hh: hg0 // HGRN_DV + hh),
         (hf, HGRN_DK, lambda hh: hh), (hf, HGRN_DK, lambda hh: HGRN_HEADS + hh)],
        [(lw["hgrn_log_lb"], (1, 1, HGRN_DK), lambda i, hh: (hh, 0, 0)),
         (lw["hgrn_log_1mlb"], (1, 1, HGRN_DK), lambda i, hh: (hh, 0, 0)),
         (lw["hgrn_1mlb"], (1, 1, HGRN_DK), lambda i, hh: (hh, 0, 0)),
         (lw["hgrn_onorm_g"], (1, HGRN_DV), lambda i, hh: (0, 0))],
        dims, HGRN_HEADS, HGRN_DV)

    if with_ctx:
        o_a = jnp.concatenate([o_a_ctx, o_a_lat], axis=0)
        o_b = jnp.concatenate([o_b_ctx, o_b_lat], axis=0)
        o_c = jnp.concatenate([o_c_ctx, o_c_lat], axis=0)
        rows, row0 = n, 0
    else:
        o_a, o_b, o_c = o_a_lat, o_b_lat, o_c_lat
        rows, row0 = nl, nc
    tmo = min(512, tile)
    layer = lw["layer"]
    m = _merge(o_a, o_b, o_c, lw["w_branch"], layer, big, z0, rows, row0, tmo, d)
    tno = min(1024, d)
    x1, = _mm(m, lw["w_out"], _ep_residual, tm=tmo, tn=tno, rows=rows, ncols=d, w_blk0=layer,
              extras=[(xg, (tmo, tno), lambda i, j: (i + row0 // tmo, j)),
                      (modv, (1, 1, tno), lambda i, j: mod_blk(2, tmo, row0)(i)[:1] + (0, j))],
              outs=[(jax.ShapeDtypeStruct((rows, d), f32), (tmo, tno), lambda i, j: (i, j))],
              name="out_proj_residual")

    h2 = norm_mod(x1, lw["norm2_g"], 3, 4, rows, row0, f32)
    tmr = min(512, tile)
    logits, = _mm(h2, lw["w_router"], _ep_bias, tm=tmr, tn=LANES, rows=rows, ncols=LANES,
                  extras=[(lw["b_router"], (1, LANES), lambda i, j: (0, 0))],
                  outs=[(jax.ShapeDtypeStruct((rows, LANES), f32), (tmr, LANES), row_tile)],
                  name="moe_router")
    expert, weight = _moe_route(logits)
    slot_dst, block_expert, n_used = _moe_dispatch(expert)
    y2 = _moe_experts(h2, slot_dst, block_expert + layer * MOE_EXPERTS, n_used,
                      lw["moe_w_gate"], lw["moe_w_up"], lw["moe_w_down"])

    tmc = min(256, tile)
    in_specs = [pl.BlockSpec((tmc, d), lambda i: (i, 0)),
                pl.BlockSpec((tmc, d), lambda i: (i, 0)),
                pl.BlockSpec((tmc, d), lambda i: (rows // tmc + i, 0)),
                pl.BlockSpec((tmc, MOE_TOP_K), lambda i: (i, 0)),
                pl.BlockSpec((1, 1, d), mod_blk(5, tmc, row0))]
    args = [x1, y2, y2, weight, modv]
    if final_g is None:
        body, name = _combine_body, "moe_combine"
    else:
        body, name = _final_norm_body, "moe_combine_final_norm"
        in_specs.append(pl.BlockSpec((1, d), lambda i: (0, 0)))
        args.append(final_g.reshape(1, d))
    return pl.pallas_call(
        body,
        grid=(rows // tmc,),
        in_specs=in_specs,
        out_specs=pl.BlockSpec((tmc, d), lambda i: (i, 0)),
        out_shape=jax.ShapeDtypeStruct((rows, d), f32),
        compiler_params=_params(1),
        name=name,
    )(*args)


def _layer_weights(layer, d, w_in, mla_q_norm_g, mla_kv_norm_g, mla_w_uq, mla_w_ukv, gla_w_a2, gla_b_a,
                   gla_onorm_g, hgrn_lb, hgrn_onorm_g, w_branch, w_out, norm1_g, norm2_g,
                   moe_w_group, moe_b_group, moe_w_expert, moe_b_expert, moe_w_gate, moe_w_up, moe_w_down):
    widths = (MLA_Q_RANK, MLA_KV_RANK, MLA_ROPE,
              GLA_HEADS * GLA_DK, GLA_HEADS * GLA_DK, GLA_HEADS * GLA_DV, GLA_HEADS * GLA_DV,
              GLA_GATE_RANK, GLA_GATE_RANK,
              HGRN_HEADS * HGRN_DK, HGRN_HEADS * HGRN_DV, HGRN_HEADS * HGRN_DK, HGRN_HEADS * HGRN_DK,
              HGRN_HEADS * HGRN_DV, d, d, d)
    offs = [0]
    for w in widths:
        offs.append(offs[-1] + w)
    assert offs[-1] == w_in.shape[2]
    def col_t(a, e):
        piece = lax.slice(w_in, (layer, 0, offs[a]), (layer + 1, d, offs[e]))
        return jnp.swapaxes(piece, 1, 2).reshape(offs[e] - offs[a], d)

    w_kr = col_t(2, 3)
    w_small = jnp.concatenate(
        [w_kr, _rope_swap_cols(w_kr.T).T, col_t(7, 9),
         jnp.zeros((2 * LANES - 2 * MLA_ROPE - 2 * GLA_GATE_RANK, d), f32)], axis=0)
    w_big = jnp.concatenate([col_t(3, 7), col_t(9, 11), col_t(13, 14), col_t(14, 17)], axis=0)
    uq = mla_w_uq[layer]
    uq_rope = uq[..., MLA_NOPE:]
    w_uq = jnp.concatenate([uq[..., :MLA_NOPE], uq_rope, _rope_swap_cols(uq_rope)], axis=-1)
    wa = jnp.zeros((2, LANES, GLA_HEADS * GLA_DK), f32)
    wa = wa.at[0, :GLA_GATE_RANK].set(gla_w_a2[layer, 0])
    wa = wa.at[1, GLA_GATE_RANK:2 * GLA_GATE_RANK].set(gla_w_a2[layer, 1])
    wa = wa.reshape(2, LANES, GLA_HEADS, GLA_DK).transpose(2, 0, 1, 3)
    ba = gla_b_a[layer].reshape(2, GLA_HEADS, 1, GLA_DK).transpose(1, 0, 2, 3)
    lb = hgrn_lb[layer].reshape(HGRN_HEADS, 1, HGRN_DK)
    w_router = jnp.concatenate(
        [moe_w_group[layer], moe_w_expert[layer], jnp.zeros((d, LANES - MOE_GROUPS - MOE_EXPERTS), f32)], axis=-1)
    b_router = jnp.concatenate(
        [moe_b_group[layer], moe_b_expert[layer], jnp.zeros((LANES - MOE_GROUPS - MOE_EXPERTS,), f32)])
    return {
        "norm1_g": norm1_g[layer], "norm2_g": norm2_g[layer],
        "w_cq": col_t(0, 1).astype(bf16), "w_ckv": col_t(1, 2).astype(bf16),
        "w_small": w_small.astype(bf16), "w_big": w_big.astype(bf16), "w_hf": col_t(11, 13).astype(bf16),
        "q_norm_g": mla_q_norm_g[layer].reshape(1, -1), "kv_norm_g": mla_kv_norm_g[layer].reshape(1, -1),
        "w_uq": w_uq.reshape(MLA_Q_RANK, MLA_HEADS * MLA_QK).astype(bf16),
        "w_ukv": mla_w_ukv[layer].reshape(MLA_KV_RANK, MLA_HEADS * (MLA_NOPE + MLA_V)).astype(bf16),
        "gla_wa": wa.astype(bf16), "gla_ba": ba, "gla_onorm_g": gla_onorm_g[layer].reshape(1, -1),
        "hgrn_log_lb": jnp.log(lb) * LOG2E, "hgrn_log_1mlb": jnp.log1p(-lb) * LOG2E, "hgrn_1mlb": 1.0 - lb,
        "hgrn_onorm_g": hgrn_onorm_g[layer].reshape(1, -1),
        "w_router": w_router.astype(bf16), "b_router": b_router.reshape(1, LANES),
        "layer": layer, "w_branch": w_branch, "w_out": w_out,
        "moe_w_gate": moe_w_gate, "moe_w_up": moe_w_up, "moe_w_down": moe_w_down,
    }


def kernel(x, c, ctx, c_ctx, w_mod, b_mod, norm1_g, w_in, mla_q_norm_g, mla_kv_norm_g, mla_w_uq, mla_w_ukv, gla_w_a2, gla_b_a, gla_onorm_g, hgrn_lb_logits, hgrn_onorm_g, w_branch, w_out, norm2_g, moe_w_group, moe_b_group, moe_w_expert, moe_b_expert, moe_w_gate, moe_w_up, moe_w_down, final_norm_g):
    b, s, d = x.shape
    n_ctx = ctx.shape[1]
    depth = w_mod.shape[0]
    nc, nl = b * n_ctx, b * s
    assert nc % s == 0 and s % GRID_W == 0 and n_ctx % SCAN_CHUNK == 0 and s % SCAN_CHUNK == 0
    tile = 1
    while nc % (tile * 2) == 0 and s % (tile * 2) == 0:
        tile *= 2
    dims = {"B": b, "S": s, "n_ctx": n_ctx, "D": d, "Nc": nc, "Nl": nl, "N": nc + nl, "tile": tile}

    n_cond = -(-(b + 1) // 8) * 8
    cond = jnp.concatenate([c, c_ctx[None, :], jnp.zeros((n_cond - b - 1, d), f32)], axis=0)
    mod_all = _mod_vectors(cond, w_mod, b_mod)

    lb_cum = jnp.cumsum(jax.nn.softmax(hgrn_lb_logits.astype(f32), axis=0), axis=0)
    hgrn_lb = lb_cum - lb_cum[:1]

    hid = moe_w_gate.shape[-1]
    wb_all = w_branch.astype(bf16).reshape(depth * w_branch.shape[1], BRANCH_WIDTH, d)
    wo_all = w_out.astype(bf16).reshape(depth * d, d)
    wg_all = moe_w_gate.astype(bf16).reshape(depth * MOE_EXPERTS, d, hid)
    wu_all = moe_w_up.astype(bf16).reshape(depth * MOE_EXPERTS, d, hid)
    wd_all = moe_w_down.astype(bf16).reshape(depth * MOE_EXPERTS, hid, d)

    xg = jnp.concatenate([ctx.reshape(nc, d), x.reshape(nl, d)], axis=0)
    for layer in range(depth):
        with_ctx = layer < depth - 1
        lw = _layer_weights(layer, d, w_in, mla_q_norm_g, mla_kv_norm_g, mla_w_uq, mla_w_ukv, gla_w_a2,
                            gla_b_a, gla_onorm_g, hgrn_lb, hgrn_onorm_g, wb_all, wo_all, norm1_g, norm2_g,
                            moe_w_group, moe_b_group, moe_w_expert, moe_b_expert, wg_all, wu_all, wd_all)
        modv = mod_all[layer, :b + 1].reshape((b + 1) * 6, 1, d)
        xg = _layer(xg, modv, lw, dims, with_ctx, None if with_ctx else final_norm_g)
    return xg.reshape(b, s, d)
```

```python
import functools

import jax
import jax.numpy as jnp
from jax import lax
from jax.experimental import pallas as pl
from jax.experimental.pallas import tpu as pltpu

f32 = jnp.float32
bf16 = jnp.bfloat16

EPS = 1e-6
LOG2E = 1.4426950408889634
ROPE_THETA = 10000.0
GRID_W = 64

MLA_HEADS = 16
MLA_Q_RANK = 1024
MLA_KV_RANK = 512
MLA_NOPE = 128
MLA_ROPE = 64
MLA_V = 128
MLA_QK = 256
GLA_HEADS = 8
GLA_DK = 128
GLA_DV = 256
GLA_GATE_RANK = 16
GLA_TAU = 16.0
HGRN_HEADS = 16
HGRN_DK = 128
HGRN_DV = 128
BRANCH_WIDTH = 2048
MOE_GROUPS = 4
MOE_EPG = 4
MOE_EXPERTS = 16
MOE_TOP_K = 2
MOE_BLOCK = 256

SCAN_CHUNK = 64
LANES = 128

VMEM_LIMIT_BYTES = 56 * 1024 * 1024


def _params(n_axes):
    return pltpu.CompilerParams(dimension_semantics=("arbitrary",) * n_axes,
                                vmem_limit_bytes=VMEM_LIMIT_BYTES)


def _dot(a, b):
    return jnp.dot(a, b, preferred_element_type=f32)


def _dot_nt(a, b):
    return lax.dot_general(a, b, (((1,), (1,)), ((), ())), preferred_element_type=f32)


def _dot_tn(a, b):
    return lax.dot_general(a, b, (((0,), (0,)), ((), ())), preferred_element_type=f32)


def _sigmoid(x):
    return 1.0 / (1.0 + jnp.exp(-x))


def _mm_body(x_ref, w_ref, *rest, n_extra, epilogue, w_is_t):
    x, w = x_ref[...].astype(bf16), w_ref[...].astype(bf16)
    epilogue(_dot_nt(x, w) if w_is_t else _dot(x, w), rest[:n_extra], rest[n_extra:])


def _mm(x, w, epilogue, *, tm, tn, rows, ncols, outs, extras=(), x_blk0=0, w_is_t=False, w_blk0=0,
        w_row_of=None, name):
    k = x.shape[1]
    assert rows % tm == 0 and ncols % tn == 0 and w.shape[1 if w_is_t else 0] % k == 0
    if w_row_of is not None:
        w_spec = pl.BlockSpec((pl.Element(tn), pl.Element(k)),
                              lambda i, j: (pl.multiple_of(w_row_of(j), 16), 0))
    elif w_is_t:
        w_spec = pl.BlockSpec((tn, k), lambda i, j: (j, 0))
    else:
        w_spec = pl.BlockSpec((k, tn), lambda i, j: (w_blk0, j))
    in_specs = [pl.BlockSpec((tm, k), lambda i, j: (i + x_blk0, 0)), w_spec]
    in_specs += [pl.BlockSpec(bs, im) for _, bs, im in extras]
    res = pl.pallas_call(
        functools.partial(_mm_body, n_extra=len(extras), epilogue=epilogue, w_is_t=w_is_t),
        grid=(rows // tm, ncols // tn),
        in_specs=in_specs,
        out_specs=[pl.BlockSpec(bs, im) for _, bs, im in outs],
        out_shape=[s for s, _, _ in outs],
        compiler_params=_params(2),
        name=name,
    )(x, w, *[a for a, _, _ in extras])
    return res


def _ep_cast(acc, extras, outs):
    outs[0][...] = acc.astype(outs[0].dtype)


def _ep_bias(acc, extras, outs):
    outs[0][...] = (acc + extras[0][...]).astype(outs[0].dtype)


def _ep_rmsnorm(acc, extras, outs):
    ms = jnp.mean(acc * acc, axis=-1, keepdims=True)
    outs[0][...] = (acc * lax.rsqrt(ms + EPS) * extras[0][...]).astype(outs[0].dtype)


def _rotate(blk, cos_t, sin_t):
    return blk * cos_t + pltpu.roll(blk, 64, axis=1) * sin_t


def _ep_rope_small(acc, extras, outs):
    cos_t, sin_t = extras[0][...], extras[1][...]
    outs[0][...] = _rotate(acc[:, :LANES], cos_t, sin_t).astype(outs[0].dtype)
    outs[1][...] = acc[:, LANES:]


def _ep_rope_q(acc, extras, outs):
    cos_t, sin_t = extras[0][...], extras[1][...]
    o = outs[0]
    for hh in range(acc.shape[1] // MLA_QK):
        c0 = hh * MLA_QK
        o[:, c0:c0 + LANES] = acc[:, c0:c0 + LANES].astype(o.dtype)
        o[:, c0 + LANES:c0 + MLA_QK] = _rotate(acc[:, c0 + LANES:c0 + MLA_QK], cos_t, sin_t).astype(o.dtype)


def _ep_residual(acc, extras, outs):
    outs[0][...] = extras[0][...] + extras[1][0] * acc


def _mod_body(c_ref, w_ref, b_ref, o_ref):
    c = c_ref[...]
    a = (c * _sigmoid(c)).astype(bf16)
    o_ref[0] = _dot(a, w_ref[0].astype(bf16)) + b_ref[0]


def _mod_vectors(cond, w_mod, b_mod):
    n_l, d, d6 = w_mod.shape
    r = cond.shape[0]
    tn = min(512, d6)
    return pl.pallas_call(
        _mod_body,
        grid=(n_l, d6 // tn),
        in_specs=[pl.BlockSpec((r, d), lambda l, j: (0, 0)),
                  pl.BlockSpec((1, d, tn), lambda l, j: (l, 0, j)),
                  pl.BlockSpec((1, 1, tn), lambda l, j: (l, 0, j))],
        out_specs=pl.BlockSpec((1, r, tn), lambda l, j: (l, 0, j)),
        out_shape=jax.ShapeDtypeStruct((n_l, r, d6), f32),
        compiler_params=_params(2),
        name="mod_vectors",
    )(cond, w_mod, b_mod.reshape(n_l, 1, d6))


def _norm_mod_body(x_ref, g_ref, shift_ref, scale_ref, o_ref):
    x = x_ref[...]
    ms = jnp.mean(x * x, axis=-1, keepdims=True)
    y = x * lax.rsqrt(ms + EPS) * g_ref[...]
    o_ref[...] = (y * (1.0 + scale_ref[0]) + shift_ref[0]).astype(o_ref.dtype)


def _moe_sum(x_ref, y0_ref, y1_ref, w_ref, gate_ref):
    w = w_ref[...]
    return x_ref[...] + gate_ref[0] * (y0_ref[...] * w[:, 0:1] + y1_ref[...] * w[:, 1:2])


def _final_norm_body(x_ref, y0_ref, y1_ref, w_ref, gate_ref, g_ref, o_ref):
    x = _moe_sum(x_ref, y0_ref, y1_ref, w_ref, gate_ref)
    ms = jnp.mean(x * x, axis=-1, keepdims=True)
    o_ref[...] = x * lax.rsqrt(ms + EPS) * g_ref[...]


def _combine_body(x_ref, y0_ref, y1_ref, w_ref, gate_ref, o_ref):
    o_ref[...] = _moe_sum(x_ref, y0_ref, y1_ref, w_ref, gate_ref)


def _attn_lat_body(q_ref, knc_ref, knl_ref, vc_ref, vl_ref, krc_ref, krl_ref, o_ref,
                   k_scr, v_scr, *, n_ctx, tq, scale):
    k_scr[:n_ctx, :LANES] = knc_ref[...]
    k_scr[:n_ctx, LANES:] = krc_ref[...]
    k_scr[n_ctx:, :LANES] = knl_ref[...]
    k_scr[n_ctx:, LANES:] = krl_ref[...]
    v_scr[:n_ctx, :] = vc_ref[...]
    v_scr[n_ctx:, :] = vl_ref[...]

    c2 = scale * LOG2E

    n_tiles = q_ref.shape[0] // tq
    group = 4 if n_tiles % 4 == 0 else 1

    def body(t, carry):
        rs = [pl.multiple_of((t * group + g) * tq, tq) for g in range(group)]
        ss = [_dot_nt(q_ref[pl.ds(r, tq), :], k_scr[...]) for r in rs]
        ms = [jnp.max(s, axis=-1, keepdims=True) for s in ss]
        ps = [jnp.exp2(s * c2 - m * c2) for s, m in zip(ss, ms)]
        ls = [jnp.sum(p, axis=-1, keepdims=True) for p in ps]
        os_ = [_dot(p.astype(bf16), v_scr[...]) for p in ps]
        for r, o, l in zip(rs, os_, ls):
            o_ref[pl.ds(r, tq), :] = (o / l).astype(o_ref.dtype)
        return carry

    lax.fori_loop(0, n_tiles // group, body, 0)


def _attn_ctx_body(q_ref, kn_ref, v_ref, kr_ref, o_ref, *, scale):
    k = jnp.concatenate([kn_ref[...], kr_ref[...]], axis=1)
    s = _dot_nt(q_ref[...], k) * scale
    m = jnp.max(s, axis=-1, keepdims=True)
    p = jnp.exp(s - m)
    l = jnp.sum(p, axis=-1, keepdims=True)
    o_ref[...] = (_dot(p.astype(bf16), v_ref[...]) / l).astype(o_ref.dtype)


def _attention(q, kv, kr, dims, with_ctx):
    b, s, n_ctx = dims["B"], dims["S"], dims["n_ctx"]
    nc, nl = dims["Nc"], dims["Nl"]
    lb0 = nc // s
    scale = (MLA_NOPE + MLA_ROPE) ** -0.5
    tq = min(256, s)
    o_lat = pl.pallas_call(
        functools.partial(_attn_lat_body, n_ctx=n_ctx, tq=tq, scale=scale),
        grid=(b, MLA_HEADS),
        in_specs=[
            pl.BlockSpec((s, MLA_QK), lambda i, h: (lb0 + i, h)),
            pl.BlockSpec((n_ctx, LANES), lambda i, h: (i, 2 * h)),
            pl.BlockSpec((s, LANES), lambda i, h: (lb0 + i, 2 * h)),
            pl.BlockSpec((n_ctx, LANES), lambda i, h: (i, 2 * h + 1)),
            pl.BlockSpec((s, LANES), lambda i, h: (lb0 + i, 2 * h + 1)),
            pl.BlockSpec((n_ctx, LANES), lambda i, h: (i, 0)),
            pl.BlockSpec((s, LANES), lambda i, h: (lb0 + i, 0)),
        ],
        out_specs=pl.BlockSpec((s, MLA_V), lambda i, h: (i, h)),
        out_shape=jax.ShapeDtypeStruct((nl, MLA_HEADS * MLA_V), bf16),
        scratch_shapes=[pltpu.VMEM((n_ctx + s, MLA_QK), bf16), pltpu.VMEM((n_ctx + s, MLA_V), bf16)],
        compiler_params=_params(2),
        name="mla_attention_latent",
    )(q, kv, kv, kv, kv, kr, kr)
    if not with_ctx:
        return None, o_lat
    o_ctx = pl.pallas_call(
        functools.partial(_attn_ctx_body, scale=scale),
        grid=(b, MLA_HEADS),
        in_specs=[
            pl.BlockSpec((n_ctx, MLA_QK), lambda i, h: (i, h)),
            pl.BlockSpec((n_ctx, LANES), lambda i, h: (i, 2 * h)),
            pl.BlockSpec((n_ctx, LANES), lambda i, h: (i, 2 * h + 1)),
            pl.BlockSpec((n_ctx, LANES), lambda i, h: (i, 0)),
        ],
        out_specs=pl.BlockSpec((n_ctx, MLA_V), lambda i, h: (i, h)),
        out_shape=jax.ShapeDtypeStruct((nc, MLA_HEADS * MLA_V), bf16),
        compiler_params=_params(2),
        name="mla_attention_context",
    )(q, kv, kv, kr)
    return o_ctx, o_lat


def _scan_levels(reverse):
    c = SCAN_CHUNK
    row = lax.broadcasted_iota(jnp.int32, (c, c), 0)
    col = lax.broadcasted_iota(jnp.int32, (c, c), 1)
    earlier = (row < col) if reverse else (row > col)
    x = row ^ col
    x = x | (x >> 1)
    x = x | (x >> 2)
    x = x | (x >> 4)
    top_bit = (x + 1) >> 1
    return jnp.where(earlier, top_bit, jnp.where(row == col, c, 0))


def _scan_chunks(items):
    c = SCAN_CHUNK
    kw = items[0][3].shape[1]
    row_k = lax.broadcasted_iota(jnp.int32, (c, kw), 0)
    sub8 = lax.broadcasted_iota(jnp.int32, (8, kw), 0)

    cums = []
    for q, k, v_bf, lg2, st, lvl, reverse in items:
        cum = lg2
        for s in (1, 2, 4):
            if reverse:
                cum = cum + jnp.where(row_k < c - s, pltpu.roll(cum, c - s, axis=0), 0.0)
            else:
                cum = cum + jnp.where(row_k >= s, pltpu.roll(cum, s, axis=0), 0.0)
        for s in (8, 16, 32):
            if reverse:
                cum = jnp.concatenate([cum[:c - s] + cum[s:], cum[c - s:]], axis=0)
            else:
                cum = jnp.concatenate([cum[:s], cum[s:] + cum[:c - s]], axis=0)
        cums.append(cum)

    operands = []
    for (q, k, v_bf, lg2, st, lvl, reverse), cum in zip(items, cums):
        tot = cum[0:1] if reverse else cum[c - 1:c]
        qe = (q * jnp.exp2(cum)).astype(bf16)
        kd = (k * jnp.exp2(tot - cum)).astype(bf16)
        levels = []
        h = c // 2
        while h >= 1:
            later = ((row_k & h) == 0) if reverse else ((row_k & h) != 0)
            if h >= 8:
                groups = []
                for r0 in range(0, c, 8):
                    blk = r0 // (2 * h) * (2 * h)
                    bnd = cum[blk + h - (0 if reverse else 1):blk + h + (1 if reverse else 0)]
                    if ((r0 - blk) >= h) != reverse:
                        groups.append(q[r0:r0 + 8] * jnp.exp2(cum[r0:r0 + 8] - bnd))
                    else:
                        groups.append(k[r0:r0 + 8] * jnp.exp2(bnd - cum[r0:r0 + 8]))
                m = jnp.concatenate(groups, axis=0)
            elif h > 1:
                groups = []
                for r0 in range(0, c, 8):
                    bnds = [jnp.broadcast_to(cum[b + h - (0 if reverse else 1):b + h + (1 if reverse else 0)],
                                             (8, kw)) for b in range(r0, r0 + 8, 2 * h)]
                    groups.append(bnds[0] if len(bnds) == 1 else jnp.where(sub8 < 4, bnds[0], bnds[1]))
                d = cum - jnp.concatenate(groups, axis=0)
                m = jnp.where(later, q, k) * jnp.exp2(-jnp.abs(d))
            else:
                m = jnp.where(later, q * jnp.exp2(lg2), k)
            levels.append((h, m.astype(bf16)))
            h //= 2
        operands.append((tot, qe, kd, st.astype(bf16), levels))

    products = []
    for (q, k, v_bf, lg2, st, lvl, reverse), (tot, qe, kd, st_bf, levels) in zip(items, operands):
        products.append((_dot_nt(qe, st_bf), _dot_tn(v_bf, kd), [(h, _dot_nt(m, m)) for h, m in levels]))

    scores = []
    for (q, k, v_bf, lg2, st, lvl, reverse), (o_inter, upd, ps) in zip(items, products):
        a = jnp.where(lvl == c, jnp.sum(q * k, axis=-1, keepdims=True), 0.0)
        for h, p in ps:
            a = jnp.where(lvl == h, p, a)
        scores.append(a.astype(bf16))

    results = []
    for (q, k, v_bf, lg2, st, lvl, reverse), (tot, _, _, _, _), (o_inter, upd, _), a in zip(
            items, operands, products, scores):
        results.append((o_inter + _dot(a, v_bf), st * jnp.exp2(tot) + upd))
    return results


def _rms_rows(x, g):
    ms = jnp.mean(x * x, axis=-1, keepdims=True)
    return x * lax.rsqrt(ms + EPS) * g


def _scan_body(*refs, mode, n_ctx, seq):
    if mode == "gla":
        (q_c, q_l, k_c, k_l, v_c, v_l, r_c, r_l, ga_c, ga_l, wa_ref, ba_ref, g_ref,
         o_c, o_l, sf_ref, sb_ref, of_ref, ob_ref, lvl_ref) = refs
    else:
        (q_c, q_l, v_c, v_l, r_c, r_l, ff_c, ff_l, fb_c, fb_l, la_ref, lc_ref, oml_ref, g_ref,
         o_c, o_l, sf_ref, sb_ref, of_ref, ob_ref, lvl_ref) = refs
    c = SCAN_CHUNK
    st_dir = (sf_ref, sb_ref)
    o_dir = (of_ref, ob_ref)
    for d in range(2):
        st_dir[d][...] = jnp.zeros_like(st_dir[d])
        lvl_ref[d] = _scan_levels(bool(d))

    def chunk_inputs(seg, r, d):
        q = (q_l if seg else q_c)[pl.ds(r, c), :].astype(f32)
        v = (v_l if seg else v_c)[pl.ds(r, c), :]
        if mode == "gla":
            k = (k_l if seg else k_c)[pl.ds(r, c), :].astype(f32)
            ga = (ga_l if seg else ga_c)[pl.ds(r, c), :].astype(bf16)
            x2 = (_dot(ga, wa_ref[0, d]) + ba_ref[0, d]) * LOG2E
            lg2 = (jnp.minimum(x2, 0.0) - jnp.log2(1.0 + jnp.exp2(-jnp.abs(x2)))) * (1.0 / GLA_TAU)
            q = q * (GLA_DK ** -0.5)
        else:
            fref = ((ff_l if seg else ff_c), (fb_l if seg else fb_c))[d]
            f2 = fref[pl.ds(r, c), :] * LOG2E
            ls2 = jnp.minimum(f2, 0.0) - jnp.log2(1.0 + jnp.exp2(-jnp.abs(f2)))
            la2 = la_ref[0]
            bb2 = lc_ref[0] + ls2
            lg2 = jnp.maximum(la2, bb2) + jnp.log2(1.0 + jnp.exp2(-jnp.abs(la2 - bb2)))
            k = oml_ref[0] * jnp.exp2(ls2 - f2)
        return q, k, v, lg2

    for seg, (n_rows, base) in enumerate(((n_ctx, 0), (seq, n_ctx))):
        n_chunks = n_rows // c

        def body(i, carry, seg=seg, n_chunks=n_chunks, base=base):
            rows, items = [], []
            for d in range(2):
                ci = (n_chunks - 1 - i) if d else i
                r = pl.multiple_of(ci * c, c)
                rows.append(r)
                items.append(chunk_inputs(seg, r, d) + (st_dir[d][...], lvl_ref[d], bool(d)))
            for d, (o, st_new) in enumerate(_scan_chunks(items)):
                st_dir[d][...] = st_new
                o_dir[d][pl.ds(base + rows[d], c), :] = o
            return carry

        lax.fori_loop(0, n_chunks, body, 0, unroll=8 if n_chunks % 8 == 0 else 4)

    tr = min(256, n_ctx, seq)
    g = g_ref[...]
    for seg, (n_rows, base) in enumerate(((n_ctx, 0), (seq, n_ctx))):
        r_ref = r_l if seg else r_c
        o_ref = o_l if seg else o_c

        def post(t, carry, base=base, r_ref=r_ref, o_ref=o_ref):
            r = pl.multiple_of(t * tr, tr)
            o = of_ref[pl.ds(base + r, tr), :] + ob_ref[pl.ds(base + r, tr), :]
            gate = r_ref[pl.ds(r, tr), :].astype(f32)
            if mode == "gla":
                res = gate * _sigmoid(gate) * _rms_rows(o, g)
            else:
                res = _rms_rows(o * _sigmoid(gate), g)
            o_ref[pl.ds(r, tr), :] = res.astype(o_ref.dtype)
            return carry

        lax.fori_loop(0, n_rows // tr, post, 0)


def _scan_call(mode, tok_inputs, const_inputs, dims, n_heads, dv):
    b, s, n_ctx = dims["B"], dims["S"], dims["n_ctx"]
    nc, nl = dims["Nc"], dims["Nl"]
    lb0 = nc // s
    in_specs, args = [], []
    for arr, width, colf in tok_inputs:
        in_specs.append(pl.BlockSpec((n_ctx, width), lambda i, h, colf=colf: (i, colf(h))))
        in_specs.append(pl.BlockSpec((s, width), lambda i, h, colf=colf: (lb0 + i, colf(h))))
        args += [arr, arr]
    for arr, bs, im in const_inputs:
        in_specs.append(pl.BlockSpec(bs, im))
        args.append(arr)
    return pl.pallas_call(
        functools.partial(_scan_body, mode=mode, n_ctx=n_ctx, seq=s),
        grid=(b, n_heads),
        in_specs=in_specs,
        out_specs=[pl.BlockSpec((n_ctx, dv), lambda i, h: (i, h)),
                   pl.BlockSpec((s, dv), lambda i, h: (i, h))],
        out_shape=[jax.ShapeDtypeStruct((nc, n_heads * dv), bf16),
                   jax.ShapeDtypeStruct((nl, n_heads * dv), bf16)],
        scratch_shapes=[pltpu.VMEM((dv, LANES), f32),
                        pltpu.VMEM((dv, LANES), f32),
                        pltpu.VMEM((n_ctx + s, dv), f32),
                        pltpu.VMEM((n_ctx + s, dv), f32),
                        pltpu.VMEM((2, SCAN_CHUNK, SCAN_CHUNK), jnp.int32)],
        compiler_params=_params(2),
        name="scan_" + mode,
    )(*args)


def _merge_body(oa_ref, ob_ref, oc_ref, w_ref, za_ref, zb_ref, zc_ref, m_ref):
    acc = _sigmoid(za_ref[...].astype(f32)) * _dot(oa_ref[...], w_ref[0])
    acc += _sigmoid(zb_ref[...].astype(f32)) * _dot(ob_ref[...], w_ref[1])
    acc += _sigmoid(zc_ref[...].astype(f32)) * _dot(oc_ref[...], w_ref[2])
    m_ref[...] = acc.astype(m_ref.dtype)


def _merge(o_a, o_b, o_c, w_branch, layer, big, z_col0, rows, row0, tm, d):
    tn = min(512, d)
    assert rows % tm == 0 and row0 % tm == 0 and z_col0 % tn == 0
    zb0 = z_col0 // tn
    nzb = d // tn
    rb0 = row0 // tm
    o_spec = pl.BlockSpec((tm, BRANCH_WIDTH), lambda i, j: (i, 0))
    return pl.pallas_call(
        _merge_body,
        grid=(rows // tm, d // tn),
        in_specs=[o_spec, o_spec, o_spec,
                  pl.BlockSpec((3, BRANCH_WIDTH, tn), lambda i, j: (layer, 0, j)),
                  pl.BlockSpec((tm, tn), lambda i, j: (rb0 + i, zb0 + j)),
                  pl.BlockSpec((tm, tn), lambda i, j: (rb0 + i, zb0 + nzb + j)),
                  pl.BlockSpec((tm, tn), lambda i, j: (rb0 + i, zb0 + 2 * nzb + j))],
        out_specs=pl.BlockSpec((tm, tn), lambda i, j: (i, j)),
        out_shape=jax.ShapeDtypeStruct((rows, d), bf16),
        compiler_params=_params(2),
        name="branch_merge",
    )(o_a, o_b, o_c, w_branch, big, big, big)


def _moe_body(be_ref, nused_ref, dst_ref, h_hbm, wg_ref, wu_ref, wd_ref, y_hbm,
              xbuf, ybuf, sem_in, sem_out, *, rows):
    i = pl.program_id(0)
    blk = MOE_BLOCK

    @pl.when(i == 0)
    def _():
        ybuf[...] = jnp.zeros_like(ybuf)
        spare = pltpu.make_async_copy(ybuf, y_hbm.at[pl.ds(MOE_TOP_K * rows, blk)], sem_out)
        spare.start()
        spare.wait()

    n_used = nused_ref[0]
    slot = i % 2

    def gather(step, buf):
        def issue(r, carry):
            tok = jnp.maximum(dst_ref[step * blk + r], 0) >> 1
            pltpu.make_async_copy(h_hbm.at[pl.ds(tok, 1)], xbuf.at[buf, pl.ds(r, 1)], sem_in.at[buf]).start()
            return carry

        lax.fori_loop(0, blk, issue, 0, unroll=8)

    def wait_scatter():
        pltpu.make_async_copy(ybuf, y_hbm.at[pl.ds(0, blk)], sem_out).wait()

    @pl.when(i == 0)
    def _():
        gather(0, 0)

    @pl.when(i + 1 < n_used)
    def _():
        gather(i + 1, 1 - slot)

    @pl.when(i < n_used)
    def _():
        pltpu.make_async_copy(h_hbm.at[pl.ds(0, blk)], xbuf.at[slot], sem_in.at[slot]).wait()
        x = xbuf[slot].astype(bf16)
        gate = _dot(x, wg_ref[0])
        up = _dot(x, wu_ref[0])
        act = (gate * _sigmoid(gate) * up).astype(bf16)
        y = _dot(act, wd_ref[0])

        @pl.when(i > 0)
        def _():
            wait_scatter()

        ybuf[...] = y

        def put(r, carry):
            a = dst_ref[i * blk + r]
            row = jnp.where(a >= 0, (a & 1) * rows + (a >> 1), MOE_TOP_K * rows + r)
            pltpu.make_async_copy(ybuf.at[pl.ds(r, 1)], y_hbm.at[pl.ds(row, 1)], sem_out).start()
            return carry

        lax.fori_loop(0, blk, put, 0, unroll=8)

        @pl.when(i == n_used - 1)
        def _():
            wait_scatter()


def _moe_experts(h2, slot_dst, block_expert, n_used, wg, wu, wd):
    rows, d = h2.shape
    n_blocks = block_expert.shape[0]
    hid = wg.shape[-1]
    assert MOE_TOP_K == 2
    grid_spec = pltpu.PrefetchScalarGridSpec(
        num_scalar_prefetch=3,
        grid=(n_blocks,),
        in_specs=[
            pl.BlockSpec(memory_space=pl.ANY),
            pl.BlockSpec((1, d, hid), lambda i, be, nu, ds: (be[i], 0, 0)),
            pl.BlockSpec((1, d, hid), lambda i, be, nu, ds: (be[i], 0, 0)),
            pl.BlockSpec((1, hid, d), lambda i, be, nu, ds: (be[i], 0, 0)),
        ],
        out_specs=pl.BlockSpec(memory_space=pl.ANY),
        scratch_shapes=[pltpu.VMEM((2, MOE_BLOCK, d), f32), pltpu.VMEM((MOE_BLOCK, d), f32),
                        pltpu.SemaphoreType.DMA((2,)), pltpu.SemaphoreType.DMA(())],
    )
    return pl.pallas_call(
        functools.partial(_moe_body, rows=rows),
        grid_spec=grid_spec,
        out_shape=jax.ShapeDtypeStruct((rows * MOE_TOP_K + MOE_BLOCK, d), f32),
        compiler_params=_params(1),
        name="moe_experts",
    )(block_expert, n_used, slot_dst, h2, wg, wu, wd)


def _argmax_cols(cols):
    idx = jnp.zeros(cols[0].shape, jnp.int32)
    best = cols[0]
    for j in range(1, len(cols)):
        better = cols[j] > best
        idx = jnp.where(better, j, idx)
        best = jnp.where(better, cols[j], best)
    return idx, best


def _moe_route(logits):
    g = [logits[:, j] for j in range(MOE_GROUPS)]
    g_idx, g_max = _argmax_cols(g)
    g_prob = 1.0 / sum(jnp.exp(gj - g_max) for gj in g)
    e = []
    for j in range(MOE_EPG):
        ej = logits[:, MOE_GROUPS + j]
        for grp in range(1, MOE_GROUPS):
            ej = jnp.where(g_idx == grp, logits[:, MOE_GROUPS + grp * MOE_EPG + j], ej)
        e.append(ej)
    i1, v1 = _argmax_cols(e)
    i2, v2 = _argmax_cols([jnp.where(i1 == j, -jnp.inf, e[j]) for j in range(MOE_EPG)])
    t = jnp.exp(v2 - v1)
    weight = jnp.stack([1.0 / (1.0 + t), t / (1.0 + t)], axis=-1) * g_prob[:, None]
    expert = g_idx[:, None] * MOE_EPG + jnp.stack([i1, i2], axis=-1)
    return expert, weight


def _moe_dispatch(expert):
    expert = expert.reshape(-1)
    n_assign = expert.shape[0]
    ids = jnp.arange(MOE_EXPERTS, dtype=jnp.int32)
    onehot = (expert[:, None] == ids[None, :]).astype(jnp.int32)
    csum = jnp.cumsum(onehot, axis=0)
    rank = jnp.sum(onehot * csum, axis=1) - 1
    counts = csum[-1]
    padded = (counts + MOE_BLOCK - 1) // MOE_BLOCK * MOE_BLOCK
    pad_end = jnp.cumsum(padded)
    dest = jnp.sum(onehot * (pad_end - padded)[None, :], axis=1) + rank
    n_blocks = -(-n_assign // MOE_BLOCK) + MOE_EXPERTS
    slot_dst = jnp.full((n_blocks * MOE_BLOCK,), -1, jnp.int32).at[dest].set(
        jnp.arange(n_assign, dtype=jnp.int32))
    starts = jnp.arange(n_blocks, dtype=jnp.int32) * MOE_BLOCK
    block_expert = jnp.minimum(jnp.sum((pad_end[None, :] <= starts[:, None]).astype(jnp.int32), axis=1),
                               MOE_EXPERTS - 1)
    n_used = pad_end[-1:] // MOE_BLOCK
    return slot_dst, block_expert.astype(jnp.int32), n_used.astype(jnp.int32)


def _rope_swap_cols(w):
    qd = MLA_ROPE // 4
    return jnp.concatenate([-w[..., qd:2 * qd], w[..., :qd], -w[..., 3 * qd:], w[..., 2 * qd:3 * qd]], axis=-1)


def _rope_tables(seq, tm):
    qd = MLA_ROPE // 4
    n_rows = seq // GRID_W
    inv_freq = ROPE_THETA ** (-jnp.arange(qd, dtype=f32) / qd)
    row = jnp.repeat(jnp.arange(n_rows, dtype=f32), GRID_W)
    col = jnp.tile(jnp.arange(GRID_W, dtype=f32), n_rows)
    ar, ac = row[:, None] * inv_freq, col[:, None] * inv_freq
    zeros = jnp.zeros((seq, LANES - MLA_ROPE), f32)
    cos_t = jnp.concatenate([jnp.cos(ar), jnp.cos(ar), jnp.cos(ac), jnp.cos(ac), zeros], axis=-1)
    sin_t = jnp.concatenate([jnp.sin(ar), jnp.sin(ar), jnp.sin(ac), jnp.sin(ac), zeros], axis=-1)
    ident = jnp.concatenate([jnp.ones((tm, MLA_ROPE), f32), jnp.zeros((tm, LANES - MLA_ROPE), f32)], axis=-1)
    return (jnp.concatenate([ident, cos_t], axis=0),
            jnp.concatenate([jnp.zeros((tm, LANES), f32), sin_t], axis=0))


def _layer(xg, modv, lw, dims, with_ctx, final_g):
    b, s, n_ctx, d = dims["B"], dims["S"], dims["n_ctx"], dims["D"]
    nc, nl, n = dims["Nc"], dims["Nl"], dims["N"]
    tile = dims["tile"]

    def mod_blk(which, tm, row0):
        def im(i, *_):
            g0 = i * tm + row0
            cond = jnp.where(g0 < nc, b, jnp.maximum(g0 - nc, 0) // s)
            return (cond * 6 + which, 0, 0)
        return im

    def norm_mod(x, g, which_shift, which_scale, rows, row0, dtype):
        tm = min(256, tile)
        return pl.pallas_call(
            _norm_mod_body,
            grid=(rows // tm,),
            in_specs=[pl.BlockSpec((tm, d), lambda i: (i, 0)),
                      pl.BlockSpec((1, d), lambda i: (0, 0)),
                      pl.BlockSpec((1, 1, d), mod_blk(which_shift, tm, row0)),
                      pl.BlockSpec((1, 1, d), mod_blk(which_scale, tm, row0))],
            out_specs=pl.BlockSpec((tm, d), lambda i: (i, 0)),
            out_shape=jax.ShapeDtypeStruct((rows, d), dtype),
            compiler_params=_params(1),
            name="norm_modulate",
        )(x, g.reshape(1, d), modv, modv)

    h = norm_mod(xg, lw["norm1_g"], 0, 1, n, 0, bf16)

    tm = min(1024, tile)
    row_tile = lambda i, j: (i, 0)

    def full_out(cols, dtype, tn):
        return (jax.ShapeDtypeStruct((n, cols), dtype), (tm, tn), lambda i, j: (i, j))

    w_in_t, in_base, in_offs = lw["w_in_t"], lw["in_base"], lw["in_offs"]
    cqn, = _mm(h, w_in_t, _ep_rmsnorm, tm=tm, tn=MLA_Q_RANK, rows=n, ncols=MLA_Q_RANK, w_is_t=True,
               w_row_of=lambda j: in_base + in_offs[0],
               extras=[(lw["q_norm_g"], (1, MLA_Q_RANK), lambda i, j: (0, 0))],
               outs=[full_out(MLA_Q_RANK, bf16, MLA_Q_RANK)], name="in_proj_cq")
    ckvn, = _mm(h, w_in_t, _ep_rmsnorm, tm=tm, tn=MLA_KV_RANK, rows=n, ncols=MLA_KV_RANK, w_is_t=True,
                w_row_of=lambda j: in_base + in_offs[1],
                extras=[(lw["kv_norm_g"], (1, MLA_KV_RANK), lambda i, j: (0, 0))],
                outs=[full_out(MLA_KV_RANK, bf16, MLA_KV_RANK)], name="in_proj_ckv")

    cos_t, sin_t = _rope_tables(s, tm)

    def rope_blk(i, j):
        g0 = i * tm
        return (jnp.where(g0 < nc, 0, 1 + (jnp.maximum(g0 - nc, 0) % s) // tm), 0)

    rope_extras = [(cos_t, (tm, LANES), rope_blk), (sin_t, (tm, LANES), rope_blk)]
    kr, ga = _mm(h, lw["w_small"], _ep_rope_small, tm=tm, tn=2 * LANES, rows=n, ncols=2 * LANES, w_is_t=True,
                 extras=rope_extras,
                 outs=[(jax.ShapeDtypeStruct((n, LANES), bf16), (tm, LANES), row_tile),
                       (jax.ShapeDtypeStruct((n, LANES), f32), (tm, LANES), row_tile)],
                 name="in_proj_small")
    big_runs = [(in_offs[3], in_offs[7]), (in_offs[9], in_offs[11]), (in_offs[13], in_offs[14]),
                (in_offs[14], in_offs[17])]
    n_big = sum(e - a for a, e in big_runs)
    tn_big = min(1024, d)
    assert all((e - a) % tn_big == 0 for a, e in big_runs)

    def big_row(j):
        row, t0 = None, 0
        for a, e in big_runs:
            r = in_base + a + (j - t0) * tn_big
            row = r if row is None else jnp.where(j >= t0, r, row)
            t0 += (e - a) // tn_big
        return row

    big, = _mm(h, w_in_t, _ep_cast, tm=tm, tn=tn_big, rows=n, ncols=n_big, w_is_t=True, w_row_of=big_row,
               outs=[full_out(n_big, bf16, tn_big)], name="in_proj_big")
    n_hf = in_offs[13] - in_offs[11]
    hf, = _mm(h, w_in_t, _ep_cast, tm=tm, tn=1024, rows=n, ncols=n_hf, w_is_t=True,
              w_row_of=lambda j: in_base + in_offs[11] + j * 1024,
              outs=[full_out(n_hf, f32, 1024)], name="in_proj_hf")

    qw = MLA_HEADS * MLA_QK
    q, = _mm(cqn, lw["w_uq"], _ep_rope_q, tm=tm, tn=1024, rows=n, ncols=qw,
             extras=rope_extras, outs=[full_out(qw, bf16, 1024)], name="mla_q_up")
    kv, = _mm(ckvn, lw["w_ukv"], _ep_cast, tm=tm, tn=1024, rows=n, ncols=qw,
              outs=[full_out(qw, bf16, 1024)], name="mla_kv_up")
    o_a_ctx, o_a_lat = _attention(q, kv, kr, dims, with_ctx)

    gk0 = GLA_HEADS * GLA_DK
    gv0 = 2 * gk0
    gr0 = gv0 + GLA_HEADS * GLA_DV
    hq0 = gr0 + GLA_HEADS * GLA_DV
    hi0 = hq0 + HGRN_HEADS * HGRN_DK
    hg0 = hi0 + HGRN_HEADS * HGRN_DV
    z0 = hg0 + HGRN_HEADS * HGRN_DV
    o_b_ctx, o_b_lat = _scan_call(
        "gla",
        [(big, GLA_DK, lambda hh: hh), (big, GLA_DK, lambda hh: gk0 // GLA_DK + hh),
         (big, GLA_DV, lambda hh: gv0 // GLA_DV + hh), (big, GLA_DV, lambda hh: gr0 // GLA_DV + hh),
         (ga, LANES, lambda hh: 0)],
        [(lw["gla_wa"], (1, 2, LANES, GLA_DK), lambda i, hh: (hh, 0, 0, 0)),
         (lw["gla_ba"], (1, 2, 1, GLA_DK), lambda i, hh: (hh, 0, 0, 0)),
         (lw["gla_onorm_g"], (1, GLA_DV), lambda i, hh: (0, 0))],
        dims, GLA_HEADS, GLA_DV)

    o_c_ctx, o_c_lat = _scan_call(
        "hgrn",
        [(big, HGRN_DK, lambda hh: hq0 // HGRN_DK + hh), (big, HGRN_DV, lambda hh: hi0 // HGRN_DV + hh),
         (big, HGRN_DV, lambda hh: hg0 // HGRN_DV + hh),
         (hf, HGRN_DK, lambda hh: hh), (hf, HGRN_DK, lambda hh: HGRN_HEADS + hh)],
        [(lw["hgrn_log_lb"], (1, 1, HGRN_DK), lambda i, hh: (hh, 0, 0)),
         (lw["hgrn_log_1mlb"], (1, 1, HGRN_DK), lambda i, hh: (hh, 0, 0)),
         (lw["hgrn_1mlb"], (1, 1, HGRN_DK), lambda i, hh: (hh, 0, 0)),
         (lw["hgrn_onorm_g"], (1, HGRN_DV), lambda i, hh: (0, 0))],
        dims, HGRN_HEADS, HGRN_DV)

    if with_ctx:
        o_a = jnp.concatenate([o_a_ctx, o_a_lat], axis=0)
        o_b = jnp.concatenate([o_b_ctx, o_b_lat], axis=0)
        o_c = jnp.concatenate([o_c_ctx, o_c_lat], axis=0)
        rows, row0 = n, 0
    else:
        o_a, o_b, o_c = o_a_lat, o_b_lat, o_c_lat
        rows, row0 = nl, nc
    tmo = min(512, tile)
    layer = lw["layer"]
    m = _merge(o_a, o_b, o_c, lw["w_branch"], layer, big, z0, rows, row0, tmo, d)
    tno = min(1024, d)
    x1, = _mm(m, lw["w_out"], _ep_residual, tm=tmo, tn=tno, rows=rows, ncols=d, w_blk0=layer,
              extras=[(xg, (tmo, tno), lambda i, j: (i + row0 // tmo, j)),
                      (modv, (1, 1, tno), lambda i, j: mod_blk(2, tmo, row0)(i)[:1] + (0, j))],
              outs=[(jax.ShapeDtypeStruct((rows, d), f32), (tmo, tno), lambda i, j: (i, j))],
              name="out_proj_residual")

    h2 = norm_mod(x1, lw["norm2_g"], 3, 4, rows, row0, f32)
    tmr = min(512, tile)
    logits, = _mm(h2, lw["w_router"], _ep_bias, tm=tmr, tn=LANES, rows=rows, ncols=LANES,
                  extras=[(lw["b_router"], (1, LANES), lambda i, j: (0, 0))],
                  outs=[(jax.ShapeDtypeStruct((rows, LANES), f32), (tmr, LANES), row_tile)],
                  name="moe_router")
    expert, weight = _moe_route(logits)
    slot_dst, block_expert, n_used = _moe_dispatch(expert)
    y2 = _moe_experts(h2, slot_dst, block_expert + layer * MOE_EXPERTS, n_used,
                      lw["moe_w_gate"], lw["moe_w_up"], lw["moe_w_down"])

    tmc = min(256, tile)
    in_specs = [pl.BlockSpec((tmc, d), lambda i: (i, 0)),
                pl.BlockSpec((tmc, d), lambda i: (i, 0)),
                pl.BlockSpec((tmc, d), lambda i: (rows // tmc + i, 0)),
                pl.BlockSpec((tmc, MOE_TOP_K), lambda i: (i, 0)),
                pl.BlockSpec((1, 1, d), mod_blk(5, tmc, row0))]
    args = [x1, y2, y2, weight, modv]
    if final_g is None:
        body, name = _combine_body, "moe_combine"
    else:
        body, name = _final_norm_body, "moe_combine_final_norm"
        in_specs.append(pl.BlockSpec((1, d), lambda i: (0, 0)))
        args.append(final_g.reshape(1, d))
    return pl.pallas_call(
        body,
        grid=(rows // tmc,),
        in_specs=in_specs,
        out_specs=pl.BlockSpec((tmc, d), lambda i: (i, 0)),
        out_shape=jax.ShapeDtypeStruct((rows, d), f32),
        compiler_params=_params(1),
        name=name,
    )(*args)


def _layer_weights(layer, d, w_in, w_in_t, mla_q_norm_g, mla_kv_norm_g, mla_w_uq, mla_w_ukv, gla_w_a2, gla_b_a,
                   gla_onorm_g, hgrn_lb, hgrn_onorm_g, w_branch, w_out, norm1_g, norm2_g,
                   moe_w_group, moe_b_group, moe_w_expert, moe_b_expert, moe_w_gate, moe_w_up, moe_w_down):
    widths = (MLA_Q_RANK, MLA_KV_RANK, MLA_ROPE,
              GLA_HEADS * GLA_DK, GLA_HEADS * GLA_DK, GLA_HEADS * GLA_DV, GLA_HEADS * GLA_DV,
              GLA_GATE_RANK, GLA_GATE_RANK,
              HGRN_HEADS * HGRN_DK, HGRN_HEADS * HGRN_DV, HGRN_HEADS * HGRN_DK, HGRN_HEADS * HGRN_DK,
              HGRN_HEADS * HGRN_DV, d, d, d)
    offs = [0]
    for w in widths:
        offs.append(offs[-1] + w)
    assert offs[-1] == w_in.shape[2]
    def col_t(a, e):
        piece = lax.slice(w_in, (layer, 0, offs[a]), (layer + 1, d, offs[e]))
        return jnp.swapaxes(piece, 1, 2).reshape(offs[e] - offs[a], d)

    w_kr = col_t(2, 3)
    w_small = jnp.concatenate(
        [w_kr, _rope_swap_cols(w_kr.T).T, col_t(7, 9),
         jnp.zeros((2 * LANES - 2 * MLA_ROPE - 2 * GLA_GATE_RANK, d), f32)], axis=0)
    uq = mla_w_uq[layer]
    uq_rope = uq[..., MLA_NOPE:]
    w_uq = jnp.concatenate([uq[..., :MLA_NOPE], uq_rope, _rope_swap_cols(uq_rope)], axis=-1)
    wa = jnp.zeros((2, LANES, GLA_HEADS * GLA_DK), f32)
    wa = wa.at[0, :GLA_GATE_RANK].set(gla_w_a2[layer, 0])
    wa = wa.at[1, GLA_GATE_RANK:2 * GLA_GATE_RANK].set(gla_w_a2[layer, 1])
    wa = wa.reshape(2, LANES, GLA_HEADS, GLA_DK).transpose(2, 0, 1, 3)
    ba = gla_b_a[layer].reshape(2, GLA_HEADS, 1, GLA_DK).transpose(1, 0, 2, 3)
    lb = hgrn_lb[layer].reshape(HGRN_HEADS, 1, HGRN_DK)
    w_router = jnp.concatenate(
        [moe_w_group[layer], moe_w_expert[layer], jnp.zeros((d, LANES - MOE_GROUPS - MOE_EXPERTS), f32)], axis=-1)
    b_router = jnp.concatenate(
        [moe_b_group[layer], moe_b_expert[layer], jnp.zeros((LANES - MOE_GROUPS - MOE_EXPERTS,), f32)])
    return {
        "norm1_g": norm1_g[layer], "norm2_g": norm2_g[layer],
        "w_in_t": w_in_t, "in_base": layer * offs[-1], "in_offs": offs, "w_small": w_small.astype(bf16),
        "q_norm_g": mla_q_norm_g[layer].reshape(1, -1), "kv_norm_g": mla_kv_norm_g[layer].reshape(1, -1),
        "w_uq": w_uq.reshape(MLA_Q_RANK, MLA_HEADS * MLA_QK).astype(bf16),
        "w_ukv": mla_w_ukv[layer].reshape(MLA_KV_RANK, MLA_HEADS * (MLA_NOPE + MLA_V)).astype(bf16),
        "gla_wa": wa.astype(bf16), "gla_ba": ba, "gla_onorm_g": gla_onorm_g[layer].reshape(1, -1),
        "hgrn_log_lb": jnp.log(lb) * LOG2E, "hgrn_log_1mlb": jnp.log1p(-lb) * LOG2E, "hgrn_1mlb": 1.0 - lb,
        "hgrn_onorm_g": hgrn_onorm_g[layer].reshape(1, -1),
        "w_router": w_router.astype(bf16), "b_router": b_router.reshape(1, LANES),
        "layer": layer, "w_branch": w_branch, "w_out": w_out,
        "moe_w_gate": moe_w_gate, "moe_w_up": moe_w_up, "moe_w_down": moe_w_down,
    }


def kernel(x, c, ctx, c_ctx, w_mod, b_mod, norm1_g, w_in, mla_q_norm_g, mla_kv_norm_g, mla_w_uq, mla_w_ukv, gla_w_a2, gla_b_a, gla_onorm_g, hgrn_lb_logits, hgrn_onorm_g, w_branch, w_out, norm2_g, moe_w_group, moe_b_group, moe_w_expert, moe_b_expert, moe_w_gate, moe_w_up, moe_w_down, final_norm_g):
    b, s, d = x.shape
    n_ctx = ctx.shape[1]
    depth = w_mod.shape[0]
    nc, nl = b * n_ctx, b * s
    assert nc % s == 0 and s % GRID_W == 0 and n_ctx % SCAN_CHUNK == 0 and s % SCAN_CHUNK == 0
    tile = 1
    while nc % (tile * 2) == 0 and s % (tile * 2) == 0:
        tile *= 2
    dims = {"B": b, "S": s, "n_ctx": n_ctx, "D": d, "Nc": nc, "Nl": nl, "N": nc + nl, "tile": tile}

    n_cond = -(-(b + 1) // 8) * 8
    cond = jnp.concatenate([c, c_ctx[None, :], jnp.zeros((n_cond - b - 1, d), f32)], axis=0)
    mod_all = _mod_vectors(cond, w_mod, b_mod)

    lb_cum = jnp.cumsum(jax.nn.softmax(hgrn_lb_logits.astype(f32), axis=0), axis=0)
    hgrn_lb = lb_cum - lb_cum[:1]

    hid = moe_w_gate.shape[-1]
    w_in_t = jnp.swapaxes(w_in, 1, 2).astype(bf16).reshape(depth * w_in.shape[2], d)
    wb_all = w_branch.astype(bf16).reshape(depth * w_branch.shape[1], BRANCH_WIDTH, d)
    wo_all = w_out.astype(bf16).reshape(depth * d, d)
    wg_all = moe_w_gate.astype(bf16).reshape(depth * MOE_EXPERTS, d, hid)
    wu_all = moe_w_up.astype(bf16).reshape(depth * MOE_EXPERTS, d, hid)
    wd_all = moe_w_down.astype(bf16).reshape(depth * MOE_EXPERTS, hid, d)

    xg = jnp.concatenate([ctx.reshape(nc, d), x.reshape(nl, d)], axis=0)
    for layer in range(depth):
        with_ctx = layer < depth - 1
        lw = _layer_weights(layer, d, w_in, w_in_t, mla_q_norm_g, mla_kv_norm_g, mla_w_uq, mla_w_ukv, gla_w_a2,
                            gla_b_a, gla_onorm_g, hgrn_lb, hgrn_onorm_g, wb_all, wo_all, norm1_g, norm2_g,
                            moe_w_group, moe_b_group, moe_w_expert, moe_b_expert, wg_all, wu_all, wd_all)
        modv = mod_all[layer, :b + 1].reshape((b + 1) * 6, 1, d)
        xg = _layer(xg, modv, lw, dims, with_ctx, None if with_ctx else final_norm_g)
    return xg.reshape(b, s, d)
```

```python
import functools

import jax
import jax.numpy as jnp
from jax import lax
from jax.experimental import pallas as pl
from jax.experimental.pallas import tpu as pltpu

f32 = jnp.float32
bf16 = jnp.bfloat16

EPS = 1e-6
LOG2E = 1.4426950408889634
ROPE_THETA = 10000.0
GRID_W = 64

MLA_HEADS = 16
MLA_Q_RANK = 1024
MLA_KV_RANK = 512
MLA_NOPE = 128
MLA_ROPE = 64
MLA_V = 128
MLA_QK = 256
GLA_HEADS = 8
GLA_DK = 128
GLA_DV = 256
GLA_GATE_RANK = 16
GLA_TAU = 16.0
HGRN_HEADS = 16
HGRN_DK = 128
HGRN_DV = 128
BRANCH_WIDTH = 2048
MOE_GROUPS = 4
MOE_EPG = 4
MOE_EXPERTS = 16
MOE_TOP_K = 2
MOE_BLOCK = 256

SCAN_CHUNK = 64
LANES = 128

VMEM_LIMIT_BYTES = 56 * 1024 * 1024


def _params(n_axes):
    return pltpu.CompilerParams(dimension_semantics=("arbitrary",) * n_axes,
                                vmem_limit_bytes=VMEM_LIMIT_BYTES)


def _dot(a, b):
    return jnp.dot(a, b, preferred_element_type=f32)


def _dot_nt(a, b):
    return lax.dot_general(a, b, (((1,), (1,)), ((), ())), preferred_element_type=f32)


def _dot_tn(a, b):
    return lax.dot_general(a, b, (((0,), (0,)), ((), ())), preferred_element_type=f32)


def _sigmoid(x):
    return 1.0 / (1.0 + jnp.exp(-x))


def _mm_body(x_ref, w_ref, *rest, n_extra, epilogue, w_is_t):
    x, w = x_ref[...].astype(bf16), w_ref[...].astype(bf16)
    epilogue(_dot_nt(x, w) if w_is_t else _dot(x, w), rest[:n_extra], rest[n_extra:])


def _mm(x, w, epilogue, *, tm, tn, rows, ncols, outs, extras=(), x_blk0=0, w_is_t=False, w_blk0=0,
        w_row_of=None, name):
    k = x.shape[1]
    assert rows % tm == 0 and ncols % tn == 0 and w.shape[1 if w_is_t else 0] % k == 0
    if w_row_of is not None:
        w_spec = pl.BlockSpec((pl.Element(tn), pl.Element(k)),
                              lambda i, j: (pl.multiple_of(w_row_of(j), 16), 0))
    elif w_is_t:
        w_spec = pl.BlockSpec((tn, k), lambda i, j: (j, 0))
    else:
        w_spec = pl.BlockSpec((k, tn), lambda i, j: (w_blk0, j))
    in_specs = [pl.BlockSpec((tm, k), lambda i, j: (i + x_blk0, 0)), w_spec]
    in_specs += [pl.BlockSpec(bs, im) for _, bs, im in extras]
    res = pl.pallas_call(
        functools.partial(_mm_body, n_extra=len(extras), epilogue=epilogue, w_is_t=w_is_t),
        grid=(rows // tm, ncols // tn),
        in_specs=in_specs,
        out_specs=[pl.BlockSpec(bs, im) for _, bs, im in outs],
        out_shape=[s for s, _, _ in outs],
        compiler_params=_params(2),
        name=name,
    )(x, w, *[a for a, _, _ in extras])
    return res


def _ep_cast(acc, extras, outs):
    outs[0][...] = acc.astype(outs[0].dtype)


def _ep_bias(acc, extras, outs):
    outs[0][...] = (acc + extras[0][...]).astype(outs[0].dtype)


def _ep_rmsnorm(acc, extras, outs):
    ms = jnp.mean(acc * acc, axis=-1, keepdims=True)
    outs[0][...] = (acc * lax.rsqrt(ms + EPS) * extras[0][...]).astype(outs[0].dtype)


def _rotate(blk, cos_t, sin_t):
    return blk * cos_t + pltpu.roll(blk, 64, axis=1) * sin_t


def _ep_rope_small(acc, extras, outs):
    cos_t, sin_t = extras[0][...], extras[1][...]
    outs[0][...] = _rotate(acc[:, :LANES], cos_t, sin_t).astype(outs[0].dtype)
    outs[1][...] = acc[:, LANES:]


def _ep_rope_q(acc, extras, outs):
    cos_t, sin_t = extras[0][...], extras[1][...]
    o = outs[0]
    for hh in range(acc.shape[1] // MLA_QK):
        c0 = hh * MLA_QK
        o[:, c0:c0 + LANES] = acc[:, c0:c0 + LANES].astype(o.dtype)
        o[:, c0 + LANES:c0 + MLA_QK] = _rotate(acc[:, c0 + LANES:c0 + MLA_QK], cos_t, sin_t).astype(o.dtype)


def _ep_residual(acc, extras, outs):
    outs[0][...] = extras[0][...] + extras[1][0] * acc


def _mod_body(c_ref, w_ref, b_ref, o_ref):
    c = c_ref[...]
    a = (c * _sigmoid(c)).astype(bf16)
    o_ref[0] = _dot(a, w_ref[0].astype(bf16)) + b_ref[0]


def _mod_vectors(cond, w_mod, b_mod):
    n_l, d, d6 = w_mod.shape
    r = cond.shape[0]
    tn = min(512, d6)
    return pl.pallas_call(
        _mod_body,
        grid=(n_l, d6 // tn),
        in_specs=[pl.BlockSpec((r, d), lambda l, j: (0, 0)),
                  pl.BlockSpec((1, d, tn), lambda l, j: (l, 0, j)),
                  pl.BlockSpec((1, 1, tn), lambda l, j: (l, 0, j))],
        out_specs=pl.BlockSpec((1, r, tn), lambda l, j: (l, 0, j)),
        out_shape=jax.ShapeDtypeStruct((n_l, r, d6), f32),
        compiler_params=_params(2),
        name="mod_vectors",
    )(cond, w_mod, b_mod.reshape(n_l, 1, d6))


def _norm_mod_body(x_ref, g_ref, shift_ref, scale_ref, o_ref):
    x = x_ref[...]
    ms = jnp.mean(x * x, axis=-1, keepdims=True)
    y = x * lax.rsqrt(ms + EPS) * g_ref[...]
    o_ref[...] = (y * (1.0 + scale_ref[0]) + shift_ref[0]).astype(o_ref.dtype)


def _moe_sum(x_ref, y0_ref, y1_ref, w_ref, gate_ref):
    w = w_ref[...]
    return x_ref[...] + gate_ref[0] * (y0_ref[...] * w[:, 0:1] + y1_ref[...] * w[:, 1:2])


def _final_norm_body(x_ref, y0_ref, y1_ref, w_ref, gate_ref, g_ref, o_ref):
    x = _moe_sum(x_ref, y0_ref, y1_ref, w_ref, gate_ref)
    ms = jnp.mean(x * x, axis=-1, keepdims=True)
    o_ref[...] = x * lax.rsqrt(ms + EPS) * g_ref[...]


def _combine_body(x_ref, y0_ref, y1_ref, w_ref, gate_ref, o_ref):
    o_ref[...] = _moe_sum(x_ref, y0_ref, y1_ref, w_ref, gate_ref)


def _attn_lat_body(q_ref, knc_ref, knl_ref, vc_ref, vl_ref, krc_ref, krl_ref, o_ref,
                   k_scr, v_scr, *, n_ctx, tq, scale):
    k_scr[:n_ctx, :LANES] = knc_ref[...]
    k_scr[:n_ctx, LANES:] = krc_ref[...]
    k_scr[n_ctx:, :LANES] = knl_ref[...]
    k_scr[n_ctx:, LANES:] = krl_ref[...]
    v_scr[:n_ctx, :] = vc_ref[...]
    v_scr[n_ctx:, :] = vl_ref[...]

    c2 = scale * LOG2E

    n_tiles = q_ref.shape[0] // tq
    group = 4 if n_tiles % 4 == 0 else 1

    def body(t, carry):
        rs = [pl.multiple_of((t * group + g) * tq, tq) for g in range(group)]
        ss = [_dot_nt(q_ref[pl.ds(r, tq), :], k_scr[...]) for r in rs]
        ms = [jnp.max(s, axis=-1, keepdims=True) for s in ss]
        ps = [jnp.exp2(s * c2 - m * c2) for s, m in zip(ss, ms)]
        ls = [jnp.sum(p, axis=-1, keepdims=True) for p in ps]
        os_ = [_dot(p.astype(bf16), v_scr[...]) for p in ps]
        for r, o, l in zip(rs, os_, ls):
            o_ref[pl.ds(r, tq), :] = (o / l).astype(o_ref.dtype)
        return carry

    lax.fori_loop(0, n_tiles // group, body, 0)


def _attn_ctx_body(q_ref, kn_ref, v_ref, kr_ref, o_ref, *, scale):
    k = jnp.concatenate([kn_ref[...], kr_ref[...]], axis=1)
    s = _dot_nt(q_ref[...], k) * scale
    m = jnp.max(s, axis=-1, keepdims=True)
    p = jnp.exp(s - m)
    l = jnp.sum(p, axis=-1, keepdims=True)
    o_ref[...] = (_dot(p.astype(bf16), v_ref[...]) / l).astype(o_ref.dtype)


def _attention(q, kv, kr, dims, with_ctx):
    b, s, n_ctx = dims["B"], dims["S"], dims["n_ctx"]
    nc, nl = dims["Nc"], dims["Nl"]
    lb0 = nc // s
    scale = (MLA_NOPE + MLA_ROPE) ** -0.5
    tq = min(256, s)
    o_lat = pl.pallas_call(
        functools.partial(_attn_lat_body, n_ctx=n_ctx, tq=tq, scale=scale),
        grid=(b, MLA_HEADS),
        in_specs=[
            pl.BlockSpec((s, MLA_QK), lambda i, h: (lb0 + i, h)),
            pl.BlockSpec((n_ctx, LANES), lambda i, h: (i, 2 * h)),
            pl.BlockSpec((s, LANES), lambda i, h: (lb0 + i, 2 * h)),
            pl.BlockSpec((n_ctx, LANES), lambda i, h: (i, 2 * h + 1)),
            pl.BlockSpec((s, LANES), lambda i, h: (lb0 + i, 2 * h + 1)),
            pl.BlockSpec((n_ctx, LANES), lambda i, h: (i, 0)),
            pl.BlockSpec((s, LANES), lambda i, h: (lb0 + i, 0)),
        ],
        out_specs=pl.BlockSpec((s, MLA_V), lambda i, h: (i, h)),
        out_shape=jax.ShapeDtypeStruct((nl, MLA_HEADS * MLA_V), bf16),
        scratch_shapes=[pltpu.VMEM((n_ctx + s, MLA_QK), bf16), pltpu.VMEM((n_ctx + s, MLA_V), bf16)],
        compiler_params=_params(2),
        name="mla_attention_latent",
    )(q, kv, kv, kv, kv, kr, kr)
    if not with_ctx:
        return None, o_lat
    o_ctx = pl.pallas_call(
        functools.partial(_attn_ctx_body, scale=scale),
        grid=(b, MLA_HEADS),
        in_specs=[
            pl.BlockSpec((n_ctx, MLA_QK), lambda i, h: (i, h)),
            pl.BlockSpec((n_ctx, LANES), lambda i, h: (i, 2 * h)),
            pl.BlockSpec((n_ctx, LANES), lambda i, h: (i, 2 * h + 1)),
            pl.BlockSpec((n_ctx, LANES), lambda i, h: (i, 0)),
        ],
        out_specs=pl.BlockSpec((n_ctx, MLA_V), lambda i, h: (i, h)),
        out_shape=jax.ShapeDtypeStruct((nc, MLA_HEADS * MLA_V), bf16),
        compiler_params=_params(2),
        name="mla_attention_context",
    )(q, kv, kv, kr)
    return o_ctx, o_lat


def _scan_levels(reverse):
    c = SCAN_CHUNK
    row = lax.broadcasted_iota(jnp.int32, (c, c), 0)
    col = lax.broadcasted_iota(jnp.int32, (c, c), 1)
    earlier = (row < col) if reverse else (row > col)
    x = row ^ col
    x = x | (x >> 1)
    x = x | (x >> 2)
    x = x | (x >> 4)
    top_bit = (x + 1) >> 1
    return jnp.where(earlier, top_bit, jnp.where(row == col, c, 0))


def _scan_chunks(items):
    c = SCAN_CHUNK
    kw = items[0][3].shape[1]
    row_k = lax.broadcasted_iota(jnp.int32, (c, kw), 0)
    sub8 = lax.broadcasted_iota(jnp.int32, (8, kw), 0)

    cums = []
    for q, k, v_bf, lg2, st, lvl, reverse in items:
        cum = lg2
        for s in (1, 2, 4):
            if reverse:
                cum = cum + jnp.where(row_k < c - s, pltpu.roll(cum, c - s, axis=0), 0.0)
            else:
                cum = cum + jnp.where(row_k >= s, pltpu.roll(cum, s, axis=0), 0.0)
        for s in (8, 16, 32):
            if reverse:
                cum = jnp.concatenate([cum[:c - s] + cum[s:], cum[c - s:]], axis=0)
            else:
                cum = jnp.concatenate([cum[:s], cum[s:] + cum[:c - s]], axis=0)
        cums.append(cum)

    operands = []
    for (q, k, v_bf, lg2, st, lvl, reverse), cum in zip(items, cums):
        tot = cum[0:1] if reverse else cum[c - 1:c]
        qe = (q * jnp.exp2(cum)).astype(bf16)
        kd = (k * jnp.exp2(tot - cum)).astype(bf16)
        levels = []
        h = c // 2
        while h >= 1:
            later = ((row_k & h) == 0) if reverse else ((row_k & h) != 0)
            if h >= 8:
                groups = []
                for r0 in range(0, c, 8):
                    blk = r0 // (2 * h) * (2 * h)
                    bnd = cum[blk + h - (0 if reverse else 1):blk + h + (1 if reverse else 0)]
                    if ((r0 - blk) >= h) != reverse:
                        groups.append(q[r0:r0 + 8] * jnp.exp2(cum[r0:r0 + 8] - bnd))
                    else:
                        groups.append(k[r0:r0 + 8] * jnp.exp2(bnd - cum[r0:r0 + 8]))
                m = jnp.concatenate(groups, axis=0)
            elif h > 1:
                groups = []
                for r0 in range(0, c, 8):
                    bnds = [jnp.broadcast_to(cum[b + h - (0 if reverse else 1):b + h + (1 if reverse else 0)],
                                             (8, kw)) for b in range(r0, r0 + 8, 2 * h)]
                    groups.append(bnds[0] if len(bnds) == 1 else jnp.where(sub8 < 4, bnds[0], bnds[1]))
                d = cum - jnp.concatenate(groups, axis=0)
                m = jnp.where(later, q, k) * jnp.exp2(-jnp.abs(d))
            else:
                m = jnp.where(later, q * jnp.exp2(lg2), k)
            levels.append((h, m.astype(bf16)))
            h //= 2
        operands.append((tot, qe, kd, st.astype(bf16), levels))

    products = []
    for (q, k, v_bf, lg2, st, lvl, reverse), (tot, qe, kd, st_bf, levels) in zip(items, operands):
        products.append((_dot_nt(qe, st_bf), _dot_tn(v_bf, kd), [(h, _dot_nt(m, m)) for h, m in levels]))

    scores = []
    for (q, k, v_bf, lg2, st, lvl, reverse), (o_inter, upd, ps) in zip(items, products):
        a = jnp.where(lvl == c, jnp.sum(q * k, axis=-1, keepdims=True), 0.0)
        for h, p in ps:
            a = jnp.where(lvl == h, p, a)
        scores.append(a.astype(bf16))

    results = []
    for (q, k, v_bf, lg2, st, lvl, reverse), (tot, _, _, _, _), (o_inter, upd, _), a in zip(
            items, operands, products, scores):
        results.append((o_inter + _dot(a, v_bf), st * jnp.exp2(tot) + upd))
    return results


def _rms_rows(x, g):
    ms = jnp.mean(x * x, axis=-1, keepdims=True)
    return x * lax.rsqrt(ms + EPS) * g


def _scan_body(*refs, mode, n_ctx, seq):
    if mode == "gla":
        (q_c, q_l, k_c, k_l, v_c, v_l, r_c, r_l, ga_c, ga_l, wa_ref, ba_ref, g_ref,
         o_c, o_l, sf_ref, sb_ref, of_ref, ob_ref, lvl_ref, lg_ref) = refs
    else:
        (q_c, q_l, v_c, v_l, r_c, r_l, ff_c, ff_l, fb_c, fb_l, la_ref, lc_ref, oml_ref, g_ref,
         o_c, o_l, sf_ref, sb_ref, of_ref, ob_ref, lvl_ref) = refs
    c = SCAN_CHUNK
    st_dir = (sf_ref, sb_ref)
    o_dir = (of_ref, ob_ref)
    for d in range(2):
        st_dir[d][...] = jnp.zeros_like(st_dir[d])
        lvl_ref[d] = _scan_levels(bool(d))

    if mode == "gla":
        tg = min(256, n_ctx, seq)
        for n_rows, base, ga_ref in ((n_ctx, 0, ga_c), (seq, n_ctx, ga_l)):
            def gates(t, carry, base=base, ga_ref=ga_ref):
                r = pl.multiple_of(t * tg, tg)
                ga = ga_ref[pl.ds(r, tg), :].astype(bf16)
                for d in range(2):
                    x2 = (_dot(ga, wa_ref[0, d]) + ba_ref[0, d]) * LOG2E
                    lg_ref[d, pl.ds(base + r, tg), :] = (
                        (jnp.minimum(x2, 0.0) - jnp.log2(1.0 + jnp.exp2(-jnp.abs(x2)))) * (1.0 / GLA_TAU))
                return carry

            lax.fori_loop(0, n_rows // tg, gates, 0)

    def chunk_inputs(seg, r, d):
        q = (q_l if seg else q_c)[pl.ds(r, c), :].astype(f32)
        v = (v_l if seg else v_c)[pl.ds(r, c), :]
        if mode == "gla":
            k = (k_l if seg else k_c)[pl.ds(r, c), :].astype(f32)
            lg2 = lg_ref[d, pl.ds((n_ctx if seg else 0) + r, c), :]
            q = q * (GLA_DK ** -0.5)
        else:
            fref = ((ff_l if seg else ff_c), (fb_l if seg else fb_c))[d]
            f2 = fref[pl.ds(r, c), :] * LOG2E
            ls2 = jnp.minimum(f2, 0.0) - jnp.log2(1.0 + jnp.exp2(-jnp.abs(f2)))
            la2 = la_ref[0]
            bb2 = lc_ref[0] + ls2
            lg2 = jnp.maximum(la2, bb2) + jnp.log2(1.0 + jnp.exp2(-jnp.abs(la2 - bb2)))
            k = oml_ref[0] * jnp.exp2(ls2 - f2)
        return q, k, v, lg2

    for seg, (n_rows, base) in enumerate(((n_ctx, 0), (seq, n_ctx))):
        n_chunks = n_rows // c

        def body(i, carry, seg=seg, n_chunks=n_chunks, base=base):
            rows, items = [], []
            for d in range(2):
                ci = (n_chunks - 1 - i) if d else i
                r = pl.multiple_of(ci * c, c)
                rows.append(r)
                items.append(chunk_inputs(seg, r, d) + (st_dir[d][...], lvl_ref[d], bool(d)))
            for d, (o, st_new) in enumerate(_scan_chunks(items)):
                st_dir[d][...] = st_new
                o_dir[d][pl.ds(base + rows[d], c), :] = o
            return carry

        lax.fori_loop(0, n_chunks, body, 0, unroll=8 if n_chunks % 8 == 0 else 4)

    tr = min(256, n_ctx, seq)
    g = g_ref[...]
    for seg, (n_rows, base) in enumerate(((n_ctx, 0), (seq, n_ctx))):
        r_ref = r_l if seg else r_c
        o_ref = o_l if seg else o_c

        def post(t, carry, base=base, r_ref=r_ref, o_ref=o_ref):
            r = pl.multiple_of(t * tr, tr)
            o = of_ref[pl.ds(base + r, tr), :] + ob_ref[pl.ds(base + r, tr), :]
            gate = r_ref[pl.ds(r, tr), :].astype(f32)
            if mode == "gla":
                res = gate * _sigmoid(gate) * _rms_rows(o, g)
            else:
                res = _rms_rows(o * _sigmoid(gate), g)
            o_ref[pl.ds(r, tr), :] = res.astype(o_ref.dtype)
            return carry

        lax.fori_loop(0, n_rows // tr, post, 0)


def _scan_call(mode, tok_inputs, const_inputs, dims, n_heads, dv):
    b, s, n_ctx = dims["B"], dims["S"], dims["n_ctx"]
    nc, nl = dims["Nc"], dims["Nl"]
    lb0 = nc // s
    in_specs, args = [], []
    for arr, width, colf in tok_inputs:
        in_specs.append(pl.BlockSpec((n_ctx, width), lambda i, h, colf=colf: (i, colf(h))))
        in_specs.append(pl.BlockSpec((s, width), lambda i, h, colf=colf: (lb0 + i, colf(h))))
        args += [arr, arr]
    for arr, bs, im in const_inputs:
        in_specs.append(pl.BlockSpec(bs, im))
        args.append(arr)
    return pl.pallas_call(
        functools.partial(_scan_body, mode=mode, n_ctx=n_ctx, seq=s),
        grid=(b, n_heads),
        in_specs=in_specs,
        out_specs=[pl.BlockSpec((n_ctx, dv), lambda i, h: (i, h)),
                   pl.BlockSpec((s, dv), lambda i, h: (i, h))],
        out_shape=[jax.ShapeDtypeStruct((nc, n_heads * dv), bf16),
                   jax.ShapeDtypeStruct((nl, n_heads * dv), bf16)],
        scratch_shapes=[pltpu.VMEM((dv, LANES), f32),
                        pltpu.VMEM((dv, LANES), f32),
                        pltpu.VMEM((n_ctx + s, dv), f32),
                        pltpu.VMEM((n_ctx + s, dv), f32),
                        pltpu.VMEM((2, SCAN_CHUNK, SCAN_CHUNK), jnp.int32)]
        + ([pltpu.VMEM((2, n_ctx + s, LANES), f32)] if mode == "gla" else []),
        compiler_params=_params(2),
        name="scan_" + mode,
    )(*args)


def _merge_body(oa_ref, ob_ref, oc_ref, w_ref, za_ref, zb_ref, zc_ref, m_ref):
    acc = _sigmoid(za_ref[...].astype(f32)) * _dot(oa_ref[...], w_ref[0])
    acc += _sigmoid(zb_ref[...].astype(f32)) * _dot(ob_ref[...], w_ref[1])
    acc += _sigmoid(zc_ref[...].astype(f32)) * _dot(oc_ref[...], w_ref[2])
    m_ref[...] = acc.astype(m_ref.dtype)


def _merge(o_a, o_b, o_c, w_branch, layer, big, z_col0, rows, row0, tm, d):
    tn = min(512, d)
    assert rows % tm == 0 and row0 % tm == 0 and z_col0 % tn == 0
    zb0 = z_col0 // tn
    nzb = d // tn
    rb0 = row0 // tm
    o_spec = pl.BlockSpec((tm, BRANCH_WIDTH), lambda i, j: (i, 0))
    return pl.pallas_call(
        _merge_body,
        grid=(rows // tm, d // tn),
        in_specs=[o_spec, o_spec, o_spec,
                  pl.BlockSpec((3, BRANCH_WIDTH, tn), lambda i, j: (layer, 0, j)),
                  pl.BlockSpec((tm, tn), lambda i, j: (rb0 + i, zb0 + j)),
                  pl.BlockSpec((tm, tn), lambda i, j: (rb0 + i, zb0 + nzb + j)),
                  pl.BlockSpec((tm, tn), lambda i, j: (rb0 + i, zb0 + 2 * nzb + j))],
        out_specs=pl.BlockSpec((tm, tn), lambda i, j: (i, j)),
        out_shape=jax.ShapeDtypeStruct((rows, d), bf16),
        compiler_params=_params(2),
        name="branch_merge",
    )(o_a, o_b, o_c, w_branch, big, big, big)


def _moe_body(be_ref, nused_ref, dst_ref, h_hbm, wg_ref, wu_ref, wd_ref, y_hbm,
              xbuf, ybuf, sem_in, sem_out, *, rows):
    i = pl.program_id(0)
    blk = MOE_BLOCK

    @pl.when(i == 0)
    def _():
        ybuf[...] = jnp.zeros_like(ybuf)
        spare = pltpu.make_async_copy(ybuf, y_hbm.at[pl.ds(MOE_TOP_K * rows, blk)], sem_out)
        spare.start()
        spare.wait()

    n_used = nused_ref[0]
    slot = i % 2

    def gather(step, buf):
        def issue(r, carry):
            tok = jnp.maximum(dst_ref[step * blk + r], 0) >> 1
            pltpu.make_async_copy(h_hbm.at[pl.ds(tok, 1)], xbuf.at[buf, pl.ds(r, 1)], sem_in.at[buf]).start()
            return carry

        lax.fori_loop(0, blk, issue, 0, unroll=8)

    def wait_scatter():
        pltpu.make_async_copy(ybuf, y_hbm.at[pl.ds(0, blk)], sem_out).wait()

    @pl.when(i == 0)
    def _():
        gather(0, 0)

    @pl.when(i + 1 < n_used)
    def _():
        gather(i + 1, 1 - slot)

    @pl.when(i < n_used)
    def _():
        pltpu.make_async_copy(h_hbm.at[pl.ds(0, blk)], xbuf.at[slot], sem_in.at[slot]).wait()
        x = xbuf[slot].astype(bf16)
        gate = _dot(x, wg_ref[0])
        up = _dot(x, wu_ref[0])
        act = (gate * _sigmoid(gate) * up).astype(bf16)
        y = _dot(act, wd_ref[0])

        @pl.when(i > 0)
        def _():
            wait_scatter()

        ybuf[...] = y

        def put(r, carry):
            a = dst_ref[i * blk + r]
            row = jnp.where(a >= 0, (a & 1) * rows + (a >> 1), MOE_TOP_K * rows + r)
            pltpu.make_async_copy(ybuf.at[pl.ds(r, 1)], y_hbm.at[pl.ds(row, 1)], sem_out).start()
            return carry

        lax.fori_loop(0, blk, put, 0, unroll=8)

        @pl.when(i == n_used - 1)
        def _():
            wait_scatter()


def _moe_experts(h2, slot_dst, block_expert, n_used, wg, wu, wd):
    rows, d = h2.shape
    n_blocks = block_expert.shape[0]
    hid = wg.shape[-1]
    assert MOE_TOP_K == 2
    grid_spec = pltpu.PrefetchScalarGridSpec(
        num_scalar_prefetch=3,
        grid=(n_blocks,),
        in_specs=[
            pl.BlockSpec(memory_space=pl.ANY),
            pl.BlockSpec((1, d, hid), lambda i, be, nu, ds: (be[i], 0, 0)),
            pl.BlockSpec((1, d, hid), lambda i, be, nu, ds: (be[i], 0, 0)),
            pl.BlockSpec((1, hid, d), lambda i, be, nu, ds: (be[i], 0, 0)),
        ],
        out_specs=pl.BlockSpec(memory_space=pl.ANY),
        scratch_shapes=[pltpu.VMEM((2, MOE_BLOCK, d), f32), pltpu.VMEM((MOE_BLOCK, d), f32),
                        pltpu.SemaphoreType.DMA((2,)), pltpu.SemaphoreType.DMA(())],
    )
    return pl.pallas_call(
        functools.partial(_moe_body, rows=rows),
        grid_spec=grid_spec,
        out_shape=jax.ShapeDtypeStruct((rows * MOE_TOP_K + MOE_BLOCK, d), f32),
        compiler_params=_params(1),
        name="moe_experts",
    )(block_expert, n_used, slot_dst, h2, wg, wu, wd)


def _argmax_cols(cols):
    idx = jnp.zeros(cols[0].shape, jnp.int32)
    best = cols[0]
    for j in range(1, len(cols)):
        better = cols[j] > best
        idx = jnp.where(better, j, idx)
        best = jnp.where(better, cols[j], best)
    return idx, best


def _moe_route(logits):
    g = [logits[:, j] for j in range(MOE_GROUPS)]
    g_idx, g_max = _argmax_cols(g)
    g_prob = 1.0 / sum(jnp.exp(gj - g_max) for gj in g)
    e = []
    for j in range(MOE_EPG):
        ej = logits[:, MOE_GROUPS + j]
        for grp in range(1, MOE_GROUPS):
            ej = jnp.where(g_idx == grp, logits[:, MOE_GROUPS + grp * MOE_EPG + j], ej)
        e.append(ej)
    i1, v1 = _argmax_cols(e)
    i2, v2 = _argmax_cols([jnp.where(i1 == j, -jnp.inf, e[j]) for j in range(MOE_EPG)])
    t = jnp.exp(v2 - v1)
    weight = jnp.stack([1.0 / (1.0 + t), t / (1.0 + t)], axis=-1) * g_prob[:, None]
    expert = g_idx[:, None] * MOE_EPG + jnp.stack([i1, i2], axis=-1)
    return expert, weight


def _moe_dispatch(expert):
    expert = expert.reshape(-1)
    n_assign = expert.shape[0]
    ids = jnp.arange(MOE_EXPERTS, dtype=jnp.int32)
    onehot = (expert[:, None] == ids[None, :]).astype(jnp.int32)
    csum = jnp.cumsum(onehot, axis=0)
    rank = jnp.sum(onehot * csum, axis=1) - 1
    counts = csum[-1]
    padded = (counts + MOE_BLOCK - 1) // MOE_BLOCK * MOE_BLOCK
    pad_end = jnp.cumsum(padded)
    dest = jnp.sum(onehot * (pad_end - padded)[None, :], axis=1) + rank
    n_blocks = -(-n_assign // MOE_BLOCK) + MOE_EXPERTS
    slot_dst = jnp.full((n_blocks * MOE_BLOCK,), -1, jnp.int32).at[dest].set(
        jnp.arange(n_assign, dtype=jnp.int32))
    starts = jnp.arange(n_blocks, dtype=jnp.int32) * MOE_BLOCK
    block_expert = jnp.minimum(jnp.sum((pad_end[None, :] <= starts[:, None]).astype(jnp.int32), axis=1),
                               MOE_EXPERTS - 1)
    n_used = pad_end[-1:] // MOE_BLOCK
    return slot_dst, block_expert.astype(jnp.int32), n_used.astype(jnp.int32)


def _rope_swap_cols(w):
    qd = MLA_ROPE // 4
    return jnp.concatenate([-w[..., qd:2 * qd], w[..., :qd], -w[..., 3 * qd:], w[..., 2 * qd:3 * qd]], axis=-1)


def _rope_tables(seq, tm):
    qd = MLA_ROPE // 4
    n_rows = seq // GRID_W
    inv_freq = ROPE_THETA ** (-jnp.arange(qd, dtype=f32) / qd)
    row = jnp.repeat(jnp.arange(n_rows, dtype=f32), GRID_W)
    col = jnp.tile(jnp.arange(GRID_W, dtype=f32), n_rows)
    ar, ac = row[:, None] * inv_freq, col[:, None] * inv_freq
    zeros = jnp.zeros((seq, LANES - MLA_ROPE), f32)
    cos_t = jnp.concatenate([jnp.cos(ar), jnp.cos(ar), jnp.cos(ac), jnp.cos(ac), zeros], axis=-1)
    sin_t = jnp.concatenate([jnp.sin(ar), jnp.sin(ar), jnp.sin(ac), jnp.sin(ac), zeros], axis=-1)
    ident = jnp.concatenate([jnp.ones((tm, MLA_ROPE), f32), jnp.zeros((tm, LANES - MLA_ROPE), f32)], axis=-1)
    return (jnp.concatenate([ident, cos_t], axis=0),
            jnp.concatenate([jnp.zeros((tm, LANES), f32), sin_t], axis=0))


def _layer(xg, modv, lw, dims, with_ctx, final_g):
    b, s, n_ctx, d = dims["B"], dims["S"], dims["n_ctx"], dims["D"]
    nc, nl, n = dims["Nc"], dims["Nl"], dims["N"]
    tile = dims["tile"]

    def mod_blk(which, tm, row0):
        def im(i, *_):
            g0 = i * tm + row0
            cond = jnp.where(g0 < nc, b, jnp.maximum(g0 - nc, 0) // s)
            return (cond * 6 + which, 0, 0)
        return im

    def norm_mod(x, g, which_shift, which_scale, rows, row0, dtype):
        tm = min(256, tile)
        return pl.pallas_call(
            _norm_mod_body,
            grid=(rows // tm,),
            in_specs=[pl.BlockSpec((tm, d), lambda i: (i, 0)),
                      pl.BlockSpec((1, d), lambda i: (0, 0)),
                      pl.BlockSpec((1, 1, d), mod_blk(which_shift, tm, row0)),
                      pl.BlockSpec((1, 1, d), mod_blk(which_scale, tm, row0))],
            out_specs=pl.BlockSpec((tm, d), lambda i: (i, 0)),
            out_shape=jax.ShapeDtypeStruct((rows, d), dtype),
            compiler_params=_params(1),
            name="norm_modulate",
        )(x, g.reshape(1, d), modv, modv)

    h = norm_mod(xg, lw["norm1_g"], 0, 1, n, 0, bf16)

    tm = min(1024, tile)
    row_tile = lambda i, j: (i, 0)

    def full_out(cols, dtype, tn):
        return (jax.ShapeDtypeStruct((n, cols), dtype), (tm, tn), lambda i, j: (i, j))

    w_in_t, in_base, in_offs = lw["w_in_t"], lw["in_base"], lw["in_offs"]
    cqn, = _mm(h, w_in_t, _ep_rmsnorm, tm=tm, tn=MLA_Q_RANK, rows=n, ncols=MLA_Q_RANK, w_is_t=True,
               w_row_of=lambda j: in_base + in_offs[0],
               extras=[(lw["q_norm_g"], (1, MLA_Q_RANK), lambda i, j: (0, 0))],
               outs=[full_out(MLA_Q_RANK, bf16, MLA_Q_RANK)], name="in_proj_cq")
    ckvn, = _mm(h, w_in_t, _ep_rmsnorm, tm=tm, tn=MLA_KV_RANK, rows=n, ncols=MLA_KV_RANK, w_is_t=True,
                w_row_of=lambda j: in_base + in_offs[1],
                extras=[(lw["kv_norm_g"], (1, MLA_KV_RANK), lambda i, j: (0, 0))],
                outs=[full_out(MLA_KV_RANK, bf16, MLA_KV_RANK)], name="in_proj_ckv")

    cos_t, sin_t = _rope_tables(s, tm)

    def rope_blk(i, j):
        g0 = i * tm
        return (jnp.where(g0 < nc, 0, 1 + (jnp.maximum(g0 - nc, 0) % s) // tm), 0)

    rope_extras = [(cos_t, (tm, LANES), rope_blk), (sin_t, (tm, LANES), rope_blk)]
    kr, ga = _mm(h, lw["w_small"], _ep_rope_small, tm=tm, tn=2 * LANES, rows=n, ncols=2 * LANES, w_is_t=True,
                 extras=rope_extras,
                 outs=[(jax.ShapeDtypeStruct((n, LANES), bf16), (tm, LANES), row_tile),
                       (jax.ShapeDtypeStruct((n, LANES), f32), (tm, LANES), row_tile)],
                 name="in_proj_small")
    big_runs = [(in_offs[3], in_offs[7]), (in_offs[9], in_offs[11]), (in_offs[13], in_offs[14]),
                (in_offs[14], in_offs[17])]
    n_big = sum(e - a for a, e in big_runs)
    tn_big = min(1024, d)
    assert all((e - a) % tn_big == 0 for a, e in big_runs)

    def big_row(j):
        row, t0 = None, 0
        for a, e in big_runs:
            r = in_base + a + (j - t0) * tn_big
            row = r if row is None else jnp.where(j >= t0, r, row)
            t0 += (e - a) // tn_big
        return row

    big, = _mm(h, w_in_t, _ep_cast, tm=tm, tn=tn_big, rows=n, ncols=n_big, w_is_t=True, w_row_of=big_row,
               outs=[full_out(n_big, bf16, tn_big)], name="in_proj_big")
    n_hf = in_offs[13] - in_offs[11]
    hf, = _mm(h, w_in_t, _ep_cast, tm=tm, tn=1024, rows=n, ncols=n_hf, w_is_t=True,
              w_row_of=lambda j: in_base + in_offs[11] + j * 1024,
              outs=[full_out(n_hf, f32, 1024)], name="in_proj_hf")

    qw = MLA_HEADS * MLA_QK
    q, = _mm(cqn, lw["w_uq"], _ep_rope_q, tm=tm, tn=1024, rows=n, ncols=qw,
             extras=rope_extras, outs=[full_out(qw, bf16, 1024)], name="mla_q_up")
    kv, = _mm(ckvn, lw["w_ukv"], _ep_cast, tm=tm, tn=1024, rows=n, ncols=qw,
              outs=[full_out(qw, bf16, 1024)], name="mla_kv_up")
    o_a_ctx, o_a_lat = _attention(q, kv, kr, dims, with_ctx)

    gk0 = GLA_HEADS * GLA_DK
    gv0 = 2 * gk0
    gr0 = gv0 + GLA_HEADS * GLA_DV
    hq0 = gr0 + GLA_HEADS * GLA_DV
    hi0 = hq0 + HGRN_HEADS * HGRN_DK
    hg0 = hi0 + HGRN_HEADS * HGRN_DV
    z0 = hg0 + HGRN_HEADS * HGRN_DV
    o_b_ctx, o_b_lat = _scan_call(
        "gla",
        [(big, GLA_DK, lambda hh: hh), (big, GLA_DK, lambda hh: gk0 // GLA_DK + hh),
         (big, GLA_DV, lambda hh: gv0 // GLA_DV + hh), (big, GLA_DV, lambda hh: gr0 // GLA_DV + hh),
         (ga, LANES, lambda hh: 0)],
        [(lw["gla_wa"], (1, 2, LANES, GLA_DK), lambda i, hh: (hh, 0, 0, 0)),
         (lw["gla_ba"], (1, 2, 1, GLA_DK), lambda i, hh: (hh, 0, 0, 0)),
         (lw["gla_onorm_g"], (1, GLA_DV), lambda i, hh: (0, 0))],
        dims, GLA_HEADS, GLA_DV)

    o_c_ctx, o_c_lat = _scan_call(
        "hgrn",
        [(big, HGRN_DK, lambda hh: hq0 // HGRN_DK + hh), (big, HGRN_DV, lambda hh: hi0 // HGRN_DV + hh),
         (big, HGRN_DV, lambda hh: hg0 // HGRN_DV + hh),
         (hf, HGRN_DK, lambda hh: hh), (hf, HGRN_DK, lambda hh: HGRN_HEADS + hh)],
        [(lw["hgrn_log_lb"], (1, 1, HGRN_DK), lambda i, hh: (hh, 0, 0)),
         (lw["hgrn_log_1mlb"], (1, 1, HGRN_DK), lambda i, hh: (hh, 0, 0)),
         (lw["hgrn_1mlb"], (1, 1, HGRN_DK), lambda i, hh: (hh, 0, 0)),
         (lw["hgrn_onorm_g"], (1, HGRN_DV), lambda i, hh: (0, 0))],
        dims, HGRN_HEADS, HGRN_DV)

    if with_ctx:
        o_a = jnp.concatenate([o_a_ctx, o_a_lat], axis=0)
        o_b = jnp.concatenate([o_b_ctx, o_b_lat], axis=0)
        o_c = jnp.concatenate([o_c_ctx, o_c_lat], axis=0)
        rows, row0 = n, 0
    else:
        o_a, o_b, o_c = o_a_lat, o_b_lat, o_c_lat
        rows, row0 = nl, nc
    tmo = min(1024, tile)
    layer = lw["layer"]
    m = _merge(o_a, o_b, o_c, lw["w_branch"], layer, big, z0, rows, row0, tmo, d)
    tno = min(1024, d)
    x1, = _mm(m, lw["w_out"], _ep_residual, tm=tmo, tn=tno, rows=rows, ncols=d, w_blk0=layer,
              extras=[(xg, (tmo, tno), lambda i, j: (i + row0 // tmo, j)),
                      (modv, (1, 1, tno), lambda i, j: mod_blk(2, tmo, row0)(i)[:1] + (0, j))],
              outs=[(jax.ShapeDtypeStruct((rows, d), f32), (tmo, tno), lambda i, j: (i, j))],
              name="out_proj_residual")

    h2 = norm_mod(x1, lw["norm2_g"], 3, 4, rows, row0, f32)
    tmr = min(512, tile)
    logits, = _mm(h2, lw["w_router"], _ep_bias, tm=tmr, tn=LANES, rows=rows, ncols=LANES,
                  extras=[(lw["b_router"], (1, LANES), lambda i, j: (0, 0))],
                  outs=[(jax.ShapeDtypeStruct((rows, LANES), f32), (tmr, LANES), row_tile)],
                  name="moe_router")
    expert, weight = _moe_route(logits)
    slot_dst, block_expert, n_used = _moe_dispatch(expert)
    y2 = _moe_experts(h2, slot_dst, block_expert + layer * MOE_EXPERTS, n_used,
                      lw["moe_w_gate"], lw["moe_w_up"], lw["moe_w_down"])

    tmc = min(256, tile)
    in_specs = [pl.BlockSpec((tmc, d), lambda i: (i, 0)),
                pl.BlockSpec((tmc, d), lambda i: (i, 0)),
                pl.BlockSpec((tmc, d), lambda i: (rows // tmc + i, 0)),
                pl.BlockSpec((tmc, MOE_TOP_K), lambda i: (i, 0)),
                pl.BlockSpec((1, 1, d), mod_blk(5, tmc, row0))]
    args = [x1, y2, y2, weight, modv]
    if final_g is None:
        body, name = _combine_body, "moe_combine"
    else:
        body, name = _final_norm_body, "moe_combine_final_norm"
        in_specs.append(pl.BlockSpec((1, d), lambda i: (0, 0)))
        args.append(final_g.reshape(1, d))
    return pl.pallas_call(
        body,
        grid=(rows // tmc,),
        in_specs=in_specs,
        out_specs=pl.BlockSpec((tmc, d), lambda i: (i, 0)),
        out_shape=jax.ShapeDtypeStruct((rows, d), f32),
        compiler_params=_params(1),
        name=name,
    )(*args)


def _layer_weights(layer, d, w_in, w_in_t, mla_q_norm_g, mla_kv_norm_g, mla_w_uq, mla_w_ukv, gla_w_a2, gla_b_a,
                   gla_onorm_g, hgrn_lb, hgrn_onorm_g, w_branch, w_out, norm1_g, norm2_g,
                   moe_w_group, moe_b_group, moe_w_expert, moe_b_expert, moe_w_gate, moe_w_up, moe_w_down):
    widths = (MLA_Q_RANK, MLA_KV_RANK, MLA_ROPE,
              GLA_HEADS * GLA_DK, GLA_HEADS * GLA_DK, GLA_HEADS * GLA_DV, GLA_HEADS * GLA_DV,
              GLA_GATE_RANK, GLA_GATE_RANK,
              HGRN_HEADS * HGRN_DK, HGRN_HEADS * HGRN_DV, HGRN_HEADS * HGRN_DK, HGRN_HEADS * HGRN_DK,
              HGRN_HEADS * HGRN_DV, d, d, d)
    offs = [0]
    for w in widths:
        offs.append(offs[-1] + w)
    assert offs[-1] == w_in.shape[2]
    def col_t(a, e):
        piece = lax.slice(w_in, (layer, 0, offs[a]), (layer + 1, d, offs[e]))
        return jnp.swapaxes(piece, 1, 2).reshape(offs[e] - offs[a], d)

    w_kr = col_t(2, 3)
    w_small = jnp.concatenate(
        [w_kr, _rope_swap_cols(w_kr.T).T, col_t(7, 9),
         jnp.zeros((2 * LANES - 2 * MLA_ROPE - 2 * GLA_GATE_RANK, d), f32)], axis=0)
    uq = mla_w_uq[layer]
    uq_rope = uq[..., MLA_NOPE:]
    w_uq = jnp.concatenate([uq[..., :MLA_NOPE], uq_rope, _rope_swap_cols(uq_rope)], axis=-1)
    wa = jnp.zeros((2, LANES, GLA_HEADS * GLA_DK), f32)
    wa = wa.at[0, :GLA_GATE_RANK].set(gla_w_a2[layer, 0])
    wa = wa.at[1, GLA_GATE_RANK:2 * GLA_GATE_RANK].set(gla_w_a2[layer, 1])
    wa = wa.reshape(2, LANES, GLA_HEADS, GLA_DK).transpose(2, 0, 1, 3)
    ba = gla_b_a[layer].reshape(2, GLA_HEADS, 1, GLA_DK).transpose(1, 0, 2, 3)
    lb = hgrn_lb[layer].reshape(HGRN_HEADS, 1, HGRN_DK)
    w_router = jnp.concatenate(
        [moe_w_group[layer], moe_w_expert[layer], jnp.zeros((d, LANES - MOE_GROUPS - MOE_EXPERTS), f32)], axis=-1)
    b_router = jnp.concatenate(
        [moe_b_group[layer], moe_b_expert[layer], jnp.zeros((LANES - MOE_GROUPS - MOE_EXPERTS,), f32)])
    return {
        "norm1_g": norm1_g[layer], "norm2_g": norm2_g[layer],
        "w_in_t": w_in_t, "in_base": layer * offs[-1], "in_offs": offs, "w_small": w_small.astype(bf16),
        "q_norm_g": mla_q_norm_g[layer].reshape(1, -1), "kv_norm_g": mla_kv_norm_g[layer].reshape(1, -1),
        "w_uq": w_uq.reshape(MLA_Q_RANK, MLA_HEADS * MLA_QK).astype(bf16),
        "w_ukv": mla_w_ukv[layer].reshape(MLA_KV_RANK, MLA_HEADS * (MLA_NOPE + MLA_V)).astype(bf16),
        "gla_wa": wa.astype(bf16), "gla_ba": ba, "gla_onorm_g": gla_onorm_g[layer].reshape(1, -1),
        "hgrn_log_lb": jnp.log(lb) * LOG2E, "hgrn_log_1mlb": jnp.log1p(-lb) * LOG2E, "hgrn_1mlb": 1.0 - lb,
        "hgrn_onorm_g": hgrn_onorm_g[layer].reshape(1, -1),
        "w_router": w_router.astype(bf16), "b_router": b_router.reshape(1, LANES),
        "layer": layer, "w_branch": w_branch, "w_out": w_out,
        "moe_w_gate": moe_w_gate, "moe_w_up": moe_w_up, "moe_w_down": moe_w_down,
    }


def kernel(x, c, ctx, c_ctx, w_mod, b_mod, norm1_g, w_in, mla_q_norm_g, mla_kv_norm_g, mla_w_uq, mla_w_ukv, gla_w_a2, gla_b_a, gla_onorm_g, hgrn_lb_logits, hgrn_onorm_g, w_branch, w_out, norm2_g, moe_w_group, moe_b_group, moe_w_expert, moe_b_expert, moe_w_gate, moe_w_up, moe_w_down, final_norm_g):
    b, s, d = x.shape
    n_ctx = ctx.shape[1]
    depth = w_mod.shape[0]
    nc, nl = b * n_ctx, b * s
    assert nc % s == 0 and s % GRID_W == 0 and n_ctx % SCAN_CHUNK == 0 and s % SCAN_CHUNK == 0
    tile = 1
    while nc % (tile * 2) == 0 and s % (tile * 2) == 0:
        tile *= 2
    dims = {"B": b, "S": s, "n_ctx": n_ctx, "D": d, "Nc": nc, "Nl": nl, "N": nc + nl, "tile": tile}

    n_cond = -(-(b + 1) // 8) * 8
    cond = jnp.concatenate([c, c_ctx[None, :], jnp.zeros((n_cond - b - 1, d), f32)], axis=0)
    mod_all = _mod_vectors(cond, w_mod, b_mod)

    lb_cum = jnp.cumsum(jax.nn.softmax(hgrn_lb_logits.astype(f32), axis=0), axis=0)
    hgrn_lb = lb_cum - lb_cum[:1]

    hid = moe_w_gate.shape[-1]
    w_in_t = jnp.swapaxes(w_in, 1, 2).astype(bf16).reshape(depth * w_in.shape[2], d)
    wb_all = w_branch.astype(bf16).reshape(depth * w_branch.shape[1], BRANCH_WIDTH, d)
    wo_all = w_out.astype(bf16).reshape(depth * d, d)
    wg_all = moe_w_gate.astype(bf16).reshape(depth * MOE_EXPERTS, d, hid)
    wu_all = moe_w_up.astype(bf16).reshape(depth * MOE_EXPERTS, d, hid)
    wd_all = moe_w_down.astype(bf16).reshape(depth * MOE_EXPERTS, hid, d)

    xg = jnp.concatenate([ctx.reshape(nc, d), x.reshape(nl, d)], axis=0)
    for layer in range(depth):
        with_ctx = layer < depth - 1
        lw = _layer_weights(layer, d, w_in, w_in_t, mla_q_norm_g, mla_kv_norm_g, mla_w_uq, mla_w_ukv, gla_w_a2,
                            gla_b_a, gla_onorm_g, hgrn_lb, hgrn_onorm_g, wb_all, wo_all, norm1_g, norm2_g,
                            moe_w_group, moe_b_group, moe_w_expert, moe_b_expert, wg_all, wu_all, wd_all)
        modv = mod_all[layer, :b + 1].reshape((b + 1) * 6, 1, d)
        xg = _layer(xg, modv, lw, dims, with_ctx, None if with_ctx else final_norm_g)
    return xg.reshape(b, s, d)
```

```python
import functools

import jax
import jax.numpy as jnp
from jax import lax
from jax.experimental import pallas as pl
from jax.experimental.pallas import tpu as pltpu

f32 = jnp.float32
bf16 = jnp.bfloat16

EPS = 1e-6
LOG2E = 1.4426950408889634
ROPE_THETA = 10000.0
GRID_W = 64

MLA_HEADS = 16
MLA_Q_RANK = 1024
MLA_KV_RANK = 512
MLA_NOPE = 128
MLA_ROPE = 64
MLA_V = 128
MLA_QK = 256
GLA_HEADS = 8
GLA_DK = 128
GLA_DV = 256
GLA_GATE_RANK = 16
GLA_TAU = 16.0
HGRN_HEADS = 16
HGRN_DK = 128
HGRN_DV = 128
BRANCH_WIDTH = 2048
MOE_GROUPS = 4
MOE_EPG = 4
MOE_EXPERTS = 16
MOE_TOP_K = 2
MOE_BLOCK = 256

SCAN_CHUNK = 64
LANES = 128

VMEM_LIMIT_BYTES = 56 * 1024 * 1024


def _params(n_axes):
    return pltpu.CompilerParams(dimension_semantics=("arbitrary",) * n_axes,
                                vmem_limit_bytes=VMEM_LIMIT_BYTES)


def _dot(a, b):
    return jnp.dot(a, b, preferred_element_type=f32)


def _dot_nt(a, b):
    return lax.dot_general(a, b, (((1,), (1,)), ((), ())), preferred_element_type=f32)


def _dot_tn(a, b):
    return lax.dot_general(a, b, (((0,), (0,)), ((), ())), preferred_element_type=f32)


def _sigmoid(x):
    return 1.0 / (1.0 + jnp.exp(-x))


def _mm_body(x_ref, w_ref, *rest, n_extra, epilogue, w_is_t):
    x, w = x_ref[...].astype(bf16), w_ref[...].astype(bf16)
    epilogue(_dot_nt(x, w) if w_is_t else _dot(x, w), rest[:n_extra], rest[n_extra:])


def _mm(x, w, epilogue, *, tm, tn, rows, ncols, outs, extras=(), x_blk0=0, w_is_t=False, w_blk0=0,
        w_row_of=None, name):
    k = x.shape[1]
    assert rows % tm == 0 and ncols % tn == 0 and w.shape[1 if w_is_t else 0] % k == 0
    if w_row_of is not None:
        w_spec = pl.BlockSpec((pl.Element(tn), pl.Element(k)),
                              lambda i, j: (pl.multiple_of(w_row_of(j), 16), 0))
    elif w_is_t:
        w_spec = pl.BlockSpec((tn, k), lambda i, j: (j, 0))
    else:
        w_spec = pl.BlockSpec((k, tn), lambda i, j: (w_blk0, j))
    in_specs = [pl.BlockSpec((tm, k), lambda i, j: (i + x_blk0, 0)), w_spec]
    in_specs += [pl.BlockSpec(bs, im) for _, bs, im in extras]
    res = pl.pallas_call(
        functools.partial(_mm_body, n_extra=len(extras), epilogue=epilogue, w_is_t=w_is_t),
        grid=(rows // tm, ncols // tn),
        in_specs=in_specs,
        out_specs=[pl.BlockSpec(bs, im) for _, bs, im in outs],
        out_shape=[s for s, _, _ in outs],
        compiler_params=_params(2),
        name=name,
    )(x, w, *[a for a, _, _ in extras])
    return res


def _ep_cast(acc, extras, outs):
    outs[0][...] = acc.astype(outs[0].dtype)


def _ep_bias(acc, extras, outs):
    outs[0][...] = (acc + extras[0][...]).astype(outs[0].dtype)


def _ep_rmsnorm(acc, extras, outs):
    ms = jnp.mean(acc * acc, axis=-1, keepdims=True)
    outs[0][...] = (acc * lax.rsqrt(ms + EPS) * extras[0][...]).astype(outs[0].dtype)


def _rotate(blk, cos_t, sin_t):
    return blk * cos_t + pltpu.roll(blk, 64, axis=1) * sin_t


def _ep_rope_small(acc, extras, outs):
    cos_t, sin_t = extras[0][...], extras[1][...]
    outs[0][...] = _rotate(acc[:, :LANES], cos_t, sin_t).astype(outs[0].dtype)
    outs[1][...] = acc[:, LANES:]


def _ep_rope_q(acc, extras, outs):
    cos_t, sin_t = extras[0][...], extras[1][...]
    o = outs[0]
    for hh in range(acc.shape[1] // MLA_QK):
        c0 = hh * MLA_QK
        o[:, c0:c0 + LANES] = acc[:, c0:c0 + LANES].astype(o.dtype)
        o[:, c0 + LANES:c0 + MLA_QK] = _rotate(acc[:, c0 + LANES:c0 + MLA_QK], cos_t, sin_t).astype(o.dtype)


def _by_row_part(x_refs, n_first, fn):
    if len(x_refs) == 1:
        fn(x_refs[0])
    else:
        i = pl.program_id(0)
        pl.when(i < n_first)(lambda: fn(x_refs[0]))
        pl.when(i >= n_first)(lambda: fn(x_refs[1]))


def _ep_residual(acc, extras, outs, *, n_first):
    *x_refs, gate_ref = extras

    def run(x_ref):
        outs[0][...] = x_ref[...] + gate_ref[0] * acc

    _by_row_part(x_refs, n_first, run)


def _mod_body(c_ref, w_ref, b_ref, o_ref):
    c = c_ref[...]
    a = (c * _sigmoid(c)).astype(bf16)
    o_ref[0] = _dot(a, w_ref[0].astype(bf16)) + b_ref[0]


def _mod_vectors(cond, w_mod, b_mod):
    n_l, d, d6 = w_mod.shape
    r = cond.shape[0]
    tn = min(512, d6)
    return pl.pallas_call(
        _mod_body,
        grid=(n_l, d6 // tn),
        in_specs=[pl.BlockSpec((r, d), lambda l, j: (0, 0)),
                  pl.BlockSpec((1, d, tn), lambda l, j: (l, 0, j)),
                  pl.BlockSpec((1, 1, tn), lambda l, j: (l, 0, j))],
        out_specs=pl.BlockSpec((1, r, tn), lambda l, j: (l, 0, j)),
        out_shape=jax.ShapeDtypeStruct((n_l, r, d6), f32),
        compiler_params=_params(2),
        name="mod_vectors",
    )(cond, w_mod, b_mod.reshape(n_l, 1, d6))


def _norm_mod_body(*refs, n_first):
    *x_refs, g_ref, shift_ref, scale_ref, o_ref = refs

    def run(x_ref):
        x = x_ref[...]
        ms = jnp.mean(x * x, axis=-1, keepdims=True)
        y = x * lax.rsqrt(ms + EPS) * g_ref[...]
        o_ref[...] = (y * (1.0 + scale_ref[0]) + shift_ref[0]).astype(o_ref.dtype)

    _by_row_part(x_refs, n_first, run)


def _moe_sum(x_ref, y0_ref, y1_ref, w_ref, gate_ref):
    w = w_ref[...]
    return x_ref[...] + gate_ref[0] * (y0_ref[...] * w[:, 0:1] + y1_ref[...] * w[:, 1:2])


def _final_norm_body(x_ref, y0_ref, y1_ref, w_ref, gate_ref, g_ref, o_ref):
    x = _moe_sum(x_ref, y0_ref, y1_ref, w_ref, gate_ref)
    ms = jnp.mean(x * x, axis=-1, keepdims=True)
    o_ref[...] = x * lax.rsqrt(ms + EPS) * g_ref[...]


def _combine_body(x_ref, y0_ref, y1_ref, w_ref, gate_ref, o_ref):
    o_ref[...] = _moe_sum(x_ref, y0_ref, y1_ref, w_ref, gate_ref)


def _attn_lat_body(q_ref, knc_ref, knl_ref, vc_ref, vl_ref, krc_ref, krl_ref, o_ref,
                   k_scr, v_scr, *, n_ctx, tq, scale):
    k_scr[:n_ctx, :LANES] = knc_ref[...]
    k_scr[:n_ctx, LANES:] = krc_ref[...]
    k_scr[n_ctx:, :LANES] = knl_ref[...]
    k_scr[n_ctx:, LANES:] = krl_ref[...]
    v_scr[:n_ctx, :] = vc_ref[...]
    v_scr[n_ctx:, :] = vl_ref[...]

    c2 = scale * LOG2E

    n_tiles = q_ref.shape[0] // tq
    group = 4 if n_tiles % 4 == 0 else 1

    def body(t, carry):
        rs = [pl.multiple_of((t * group + g) * tq, tq) for g in range(group)]
        ss = [_dot_nt(q_ref[pl.ds(r, tq), :], k_scr[...]) for r in rs]
        ms = [jnp.max(s, axis=-1, keepdims=True) for s in ss]
        ps = [jnp.exp2(s * c2 - m * c2) for s, m in zip(ss, ms)]
        ls = [jnp.sum(p, axis=-1, keepdims=True) for p in ps]
        os_ = [_dot(p.astype(bf16), v_scr[...]) for p in ps]
        for r, o, l in zip(rs, os_, ls):
            o_ref[pl.ds(r, tq), :] = (o / l).astype(o_ref.dtype)
        return carry

    lax.fori_loop(0, n_tiles // group, body, 0)


def _attn_ctx_body(q_ref, kn_ref, v_ref, kr_ref, o_ref, *, scale):
    k = jnp.concatenate([kn_ref[...], kr_ref[...]], axis=1)
    s = _dot_nt(q_ref[...], k) * scale
    m = jnp.max(s, axis=-1, keepdims=True)
    p = jnp.exp(s - m)
    l = jnp.sum(p, axis=-1, keepdims=True)
    o_ref[...] = (_dot(p.astype(bf16), v_ref[...]) / l).astype(o_ref.dtype)


def _attention(q, kv, kr, dims, with_ctx):
    b, s, n_ctx = dims["B"], dims["S"], dims["n_ctx"]
    nc, nl = dims["Nc"], dims["Nl"]
    lb0 = nc // s
    scale = (MLA_NOPE + MLA_ROPE) ** -0.5
    tq = min(256, s)
    o_lat = pl.pallas_call(
        functools.partial(_attn_lat_body, n_ctx=n_ctx, tq=tq, scale=scale),
        grid=(b, MLA_HEADS),
        in_specs=[
            pl.BlockSpec((s, MLA_QK), lambda i, h: (lb0 + i, h)),
            pl.BlockSpec((n_ctx, LANES), lambda i, h: (i, 2 * h)),
            pl.BlockSpec((s, LANES), lambda i, h: (lb0 + i, 2 * h)),
            pl.BlockSpec((n_ctx, LANES), lambda i, h: (i, 2 * h + 1)),
            pl.BlockSpec((s, LANES), lambda i, h: (lb0 + i, 2 * h + 1)),
            pl.BlockSpec((n_ctx, LANES), lambda i, h: (i, 0)),
            pl.BlockSpec((s, LANES), lambda i, h: (lb0 + i, 0)),
        ],
        out_specs=pl.BlockSpec((s, MLA_V), lambda i, h: (i, h)),
        out_shape=jax.ShapeDtypeStruct((nl, MLA_HEADS * MLA_V), bf16),
        scratch_shapes=[pltpu.VMEM((n_ctx + s, MLA_QK), bf16), pltpu.VMEM((n_ctx + s, MLA_V), bf16)],
        compiler_params=_params(2),
        name="mla_attention_latent",
    )(q, kv, kv, kv, kv, kr, kr)
    if not with_ctx:
        return None, o_lat
    o_ctx = pl.pallas_call(
        functools.partial(_attn_ctx_body, scale=scale),
        grid=(b, MLA_HEADS),
        in_specs=[
            pl.BlockSpec((n_ctx, MLA_QK), lambda i, h: (i, h)),
            pl.BlockSpec((n_ctx, LANES), lambda i, h: (i, 2 * h)),
            pl.BlockSpec((n_ctx, LANES), lambda i, h: (i, 2 * h + 1)),
            pl.BlockSpec((n_ctx, LANES), lambda i, h: (i, 0)),
        ],
        out_specs=pl.BlockSpec((n_ctx, MLA_V), lambda i, h: (i, h)),
        out_shape=jax.ShapeDtypeStruct((nc, MLA_HEADS * MLA_V), bf16),
        compiler_params=_params(2),
        name="mla_attention_context",
    )(q, kv, kv, kr)
    return o_ctx, o_lat


def _scan_levels(reverse):
    c = SCAN_CHUNK
    row = lax.broadcasted_iota(jnp.int32, (c, c), 0)
    col = lax.broadcasted_iota(jnp.int32, (c, c), 1)
    earlier = (row < col) if reverse else (row > col)
    x = row ^ col
    x = x | (x >> 1)
    x = x | (x >> 2)
    x = x | (x >> 4)
    top_bit = (x + 1) >> 1
    return jnp.where(earlier, top_bit, jnp.where(row == col, c, 0))


def _scan_chunks(items):
    c = SCAN_CHUNK
    kw = items[0][3].shape[1]
    row_k = lax.broadcasted_iota(jnp.int32, (c, kw), 0)
    sub8 = lax.broadcasted_iota(jnp.int32, (8, kw), 0)

    cums = []
    for q, k, v_bf, lg2, st, lvl, reverse in items:
        cum = lg2
        for s in (1, 2, 4):
            if reverse:
                cum = cum + jnp.where(row_k < c - s, pltpu.roll(cum, c - s, axis=0), 0.0)
            else:
                cum = cum + jnp.where(row_k >= s, pltpu.roll(cum, s, axis=0), 0.0)
        for s in (8, 16, 32):
            if reverse:
                cum = jnp.concatenate([cum[:c - s] + cum[s:], cum[c - s:]], axis=0)
            else:
                cum = jnp.concatenate([cum[:s], cum[s:] + cum[:c - s]], axis=0)
        cums.append(cum)

    operands = []
    for (q, k, v_bf, lg2, st, lvl, reverse), cum in zip(items, cums):
        tot = cum[0:1] if reverse else cum[c - 1:c]
        qe = (q * jnp.exp2(cum)).astype(bf16)
        kd = (k * jnp.exp2(tot - cum)).astype(bf16)
        levels = []
        h = c // 2
        while h >= 1:
            later = ((row_k & h) == 0) if reverse else ((row_k & h) != 0)
            if h >= 8:
                groups = []
                for r0 in range(0, c, 8):
                    blk = r0 // (2 * h) * (2 * h)
                    bnd = cum[blk + h - (0 if reverse else 1):blk + h + (1 if reverse else 0)]
                    if ((r0 - blk) >= h) != reverse:
                        groups.append(q[r0:r0 + 8] * jnp.exp2(cum[r0:r0 + 8] - bnd))
                    else:
                        groups.append(k[r0:r0 + 8] * jnp.exp2(bnd - cum[r0:r0 + 8]))
                m = jnp.concatenate(groups, axis=0)
            elif h > 1:
                groups = []
                for r0 in range(0, c, 8):
                    bnds = [jnp.broadcast_to(cum[b + h - (0 if reverse else 1):b + h + (1 if reverse else 0)],
                                             (8, kw)) for b in range(r0, r0 + 8, 2 * h)]
                    groups.append(bnds[0] if len(bnds) == 1 else jnp.where(sub8 < 4, bnds[0], bnds[1]))
                d = cum - jnp.concatenate(groups, axis=0)
                m = jnp.where(later, q, k) * jnp.exp2(-jnp.abs(d))
            else:
                m = jnp.where(later, q * jnp.exp2(lg2), k)
            levels.append((h, m.astype(bf16)))
            h //= 2
        operands.append((tot, qe, kd, st.astype(bf16), levels))

    products = []
    for (q, k, v_bf, lg2, st, lvl, reverse), (tot, qe, kd, st_bf, levels) in zip(items, operands):
        products.append((_dot_nt(qe, st_bf), _dot_tn(v_bf, kd), [(h, _dot_nt(m, m)) for h, m in levels]))

    scores = []
    for (q, k, v_bf, lg2, st, lvl, reverse), (o_inter, upd, ps) in zip(items, products):
        a = jnp.where(lvl == c, jnp.sum(q * k, axis=-1, keepdims=True), 0.0)
        for h, p in ps:
            a = jnp.where(lvl == h, p, a)
        scores.append(a.astype(bf16))

    results = []
    for (q, k, v_bf, lg2, st, lvl, reverse), (tot, _, _, _, _), (o_inter, upd, _), a in zip(
            items, operands, products, scores):
        results.append((o_inter + _dot(a, v_bf), st * jnp.exp2(tot) + upd))
    return results


def _rms_rows(x, g):
    ms = jnp.mean(x * x, axis=-1, keepdims=True)
    return x * lax.rsqrt(ms + EPS) * g


def _scan_body(*refs, mode, n_ctx, seq):
    if mode == "gla":
        (q_c, q_l, k_c, k_l, v_c, v_l, r_c, r_l, ga_c, ga_l, wa_ref, ba_ref, g_ref,
         o_c, o_l, sf_ref, sb_ref, of_ref, ob_ref, lvl_ref, lg_ref) = refs
    else:
        (q_c, q_l, v_c, v_l, r_c, r_l, ff_c, ff_l, fb_c, fb_l, la_ref, lc_ref, oml_ref, g_ref,
         o_c, o_l, sf_ref, sb_ref, of_ref, ob_ref, lvl_ref) = refs
    c = SCAN_CHUNK
    st_dir = (sf_ref, sb_ref)
    o_dir = (of_ref, ob_ref)
    for d in range(2):
        st_dir[d][...] = jnp.zeros_like(st_dir[d])
        lvl_ref[d] = _scan_levels(bool(d))

    if mode == "gla":
        tg = min(256, n_ctx, seq)
        for n_rows, base, ga_ref in ((n_ctx, 0, ga_c), (seq, n_ctx, ga_l)):
            def gates(t, carry, base=base, ga_ref=ga_ref):
                r = pl.multiple_of(t * tg, tg)
                ga = ga_ref[pl.ds(r, tg), :].astype(bf16)
                for d in range(2):
                    x2 = (_dot(ga, wa_ref[0, d]) + ba_ref[0, d]) * LOG2E
                    lg_ref[d, pl.ds(base + r, tg), :] = (
                        (jnp.minimum(x2, 0.0) - jnp.log2(1.0 + jnp.exp2(-jnp.abs(x2)))) * (1.0 / GLA_TAU))
                return carry

            lax.fori_loop(0, n_rows // tg, gates, 0)

    def chunk_inputs(seg, r, d):
        q = (q_l if seg else q_c)[pl.ds(r, c), :].astype(f32)
        v = (v_l if seg else v_c)[pl.ds(r, c), :]
        if mode == "gla":
            k = (k_l if seg else k_c)[pl.ds(r, c), :].astype(f32)
            lg2 = lg_ref[d, pl.ds((n_ctx if seg else 0) + r, c), :]
            q = q * (GLA_DK ** -0.5)
        else:
            fref = ((ff_l if seg else ff_c), (fb_l if seg else fb_c))[d]
            f2 = fref[pl.ds(r, c), :] * LOG2E
            ls2 = jnp.minimum(f2, 0.0) - jnp.log2(1.0 + jnp.exp2(-jnp.abs(f2)))
            la2 = la_ref[0]
            bb2 = lc_ref[0] + ls2
            lg2 = jnp.maximum(la2, bb2) + jnp.log2(1.0 + jnp.exp2(-jnp.abs(la2 - bb2)))
            k = oml_ref[0] * jnp.exp2(ls2 - f2)
        return q, k, v, lg2

    for seg, (n_rows, base) in enumerate(((n_ctx, 0), (seq, n_ctx))):
        n_chunks = n_rows // c

        def body(i, carry, seg=seg, n_chunks=n_chunks, base=base):
            rows, items = [], []
            for d in range(2):
                ci = (n_chunks - 1 - i) if d else i
                r = pl.multiple_of(ci * c, c)
                rows.append(r)
                items.append(chunk_inputs(seg, r, d) + (st_dir[d][...], lvl_ref[d], bool(d)))
            for d, (o, st_new) in enumerate(_scan_chunks(items)):
                st_dir[d][...] = st_new
                o_dir[d][pl.ds(base + rows[d], c), :] = o
            return carry

        lax.fori_loop(0, n_chunks, body, 0, unroll=8 if n_chunks % 8 == 0 else 4)

    tr = min(256, n_ctx, seq)
    g = g_ref[...]
    for seg, (n_rows, base) in enumerate(((n_ctx, 0), (seq, n_ctx))):
        r_ref = r_l if seg else r_c
        o_ref = o_l if seg else o_c

        def post(t, carry, base=base, r_ref=r_ref, o_ref=o_ref):
            r = pl.multiple_of(t * tr, tr)
            o = of_ref[pl.ds(base + r, tr), :] + ob_ref[pl.ds(base + r, tr), :]
            gate = r_ref[pl.ds(r, tr), :].astype(f32)
            if mode == "gla":
                res = gate * _sigmoid(gate) * _rms_rows(o, g)
            else:
                res = _rms_rows(o * _sigmoid(gate), g)
            o_ref[pl.ds(r, tr), :] = res.astype(o_ref.dtype)
            return carry

        lax.fori_loop(0, n_rows // tr, post, 0)


def _scan_call(mode, tok_inputs, const_inputs, dims, n_heads, dv):
    b, s, n_ctx = dims["B"], dims["S"], dims["n_ctx"]
    nc, nl = dims["Nc"], dims["Nl"]
    lb0 = nc // s
    in_specs, args = [], []
    for arr, width, colf in tok_inputs:
        in_specs.append(pl.BlockSpec((n_ctx, width), lambda i, h, colf=colf: (i, colf(h))))
        in_specs.append(pl.BlockSpec((s, width), lambda i, h, colf=colf: (lb0 + i, colf(h))))
        args += [arr, arr]
    for arr, bs, im in const_inputs:
        in_specs.append(pl.BlockSpec(bs, im))
        args.append(arr)
    return pl.pallas_call(
        functools.partial(_scan_body, mode=mode, n_ctx=n_ctx, seq=s),
        grid=(b, n_heads),
        in_specs=in_specs,
        out_specs=[pl.BlockSpec((n_ctx, dv), lambda i, h: (i, h)),
                   pl.BlockSpec((s, dv), lambda i, h: (i, h))],
        out_shape=[jax.ShapeDtypeStruct((nc, n_heads * dv), bf16),
                   jax.ShapeDtypeStruct((nl, n_heads * dv), bf16)],
        scratch_shapes=[pltpu.VMEM((dv, LANES), f32),
                        pltpu.VMEM((dv, LANES), f32),
                        pltpu.VMEM((n_ctx + s, dv), f32),
                        pltpu.VMEM((n_ctx + s, dv), f32),
                        pltpu.VMEM((2, SCAN_CHUNK, SCAN_CHUNK), jnp.int32)]
        + ([pltpu.VMEM((2, n_ctx + s, LANES), f32)] if mode == "gla" else []),
        compiler_params=_params(2),
        name="scan_" + mode,
    )(*args)


def _merge_body(oa_ref, ob_ref, oc_ref, w_ref, za_ref, zb_ref, zc_ref, m_ref):
    acc = _sigmoid(za_ref[...].astype(f32)) * _dot(oa_ref[...], w_ref[0])
    acc += _sigmoid(zb_ref[...].astype(f32)) * _dot(ob_ref[...], w_ref[1])
    acc += _sigmoid(zc_ref[...].astype(f32)) * _dot(oc_ref[...], w_ref[2])
    m_ref[...] = acc.astype(m_ref.dtype)


def _merge(o_a, o_b, o_c, w_branch, layer, big, z_col0, rows, row0, tm, d):
    tn = min(512, d)
    assert rows % tm == 0 and row0 % tm == 0 and z_col0 % tn == 0
    zb0 = z_col0 // tn
    nzb = d // tn
    rb0 = row0 // tm
    o_spec = pl.BlockSpec((tm, BRANCH_WIDTH), lambda i, j: (i, 0))
    return pl.pallas_call(
        _merge_body,
        grid=(rows // tm, d // tn),
        in_specs=[o_spec, o_spec, o_spec,
                  pl.BlockSpec((3, BRANCH_WIDTH, tn), lambda i, j: (layer, 0, j)),
                  pl.BlockSpec((tm, tn), lambda i, j: (rb0 + i, zb0 + j)),
                  pl.BlockSpec((tm, tn), lambda i, j: (rb0 + i, zb0 + nzb + j)),
                  pl.BlockSpec((tm, tn), lambda i, j: (rb0 + i, zb0 + 2 * nzb + j))],
        out_specs=pl.BlockSpec((tm, tn), lambda i, j: (i, j)),
        out_shape=jax.ShapeDtypeStruct((rows, d), bf16),
        compiler_params=_params(2),
        name="branch_merge",
    )(o_a, o_b, o_c, w_branch, big, big, big)


def _moe_body(be_ref, nused_ref, dst_ref, h_hbm, wg_ref, wu_ref, wd_ref, y_hbm,
              xbuf, ybuf, sem_in, sem_out, *, rows):
    i = pl.program_id(0)
    blk = MOE_BLOCK

    @pl.when(i == 0)
    def _():
        ybuf[...] = jnp.zeros_like(ybuf)
        spare = pltpu.make_async_copy(ybuf, y_hbm.at[pl.ds(MOE_TOP_K * rows, blk)], sem_out)
        spare.start()
        spare.wait()

    n_used = nused_ref[0]
    slot = i % 2

    def gather(step, buf):
        def issue(r, carry):
            tok = jnp.maximum(dst_ref[step * blk + r], 0) >> 1
            pltpu.make_async_copy(h_hbm.at[pl.ds(tok, 1)], xbuf.at[buf, pl.ds(r, 1)], sem_in.at[buf]).start()
            return carry

        lax.fori_loop(0, blk, issue, 0, unroll=8)

    def wait_scatter():
        pltpu.make_async_copy(ybuf, y_hbm.at[pl.ds(0, blk)], sem_out).wait()

    @pl.when(i == 0)
    def _():
        gather(0, 0)

    @pl.when(i + 1 < n_used)
    def _():
        gather(i + 1, 1 - slot)

    @pl.when(i < n_used)
    def _():
        pltpu.make_async_copy(h_hbm.at[pl.ds(0, blk)], xbuf.at[slot], sem_in.at[slot]).wait()
        x = xbuf[slot].astype(bf16)
        gate = _dot(x, wg_ref[0])
        up = _dot(x, wu_ref[0])
        act = (gate * _sigmoid(gate) * up).astype(bf16)
        y = _dot(act, wd_ref[0])

        @pl.when(i > 0)
        def _():
            wait_scatter()

        ybuf[...] = y

        def put(r, carry):
            a = dst_ref[i * blk + r]
            row = jnp.where(a >= 0, (a & 1) * rows + (a >> 1), MOE_TOP_K * rows + r)
            pltpu.make_async_copy(ybuf.at[pl.ds(r, 1)], y_hbm.at[pl.ds(row, 1)], sem_out).start()
            return carry

        lax.fori_loop(0, blk, put, 0, unroll=8)

        @pl.when(i == n_used - 1)
        def _():
            wait_scatter()


def _moe_experts(h2, slot_dst, block_expert, n_used, wg, wu, wd):
    rows, d = h2.shape
    n_blocks = block_expert.shape[0]
    hid = wg.shape[-1]
    assert MOE_TOP_K == 2
    grid_spec = pltpu.PrefetchScalarGridSpec(
        num_scalar_prefetch=3,
        grid=(n_blocks,),
        in_specs=[
            pl.BlockSpec(memory_space=pl.ANY),
            pl.BlockSpec((1, d, hid), lambda i, be, nu, ds: (be[i], 0, 0)),
            pl.BlockSpec((1, d, hid), lambda i, be, nu, ds: (be[i], 0, 0)),
            pl.BlockSpec((1, hid, d), lambda i, be, nu, ds: (be[i], 0, 0)),
        ],
        out_specs=pl.BlockSpec(memory_space=pl.ANY),
        scratch_shapes=[pltpu.VMEM((2, MOE_BLOCK, d), f32), pltpu.VMEM((MOE_BLOCK, d), f32),
                        pltpu.SemaphoreType.DMA((2,)), pltpu.SemaphoreType.DMA(())],
    )
    return pl.pallas_call(
        functools.partial(_moe_body, rows=rows),
        grid_spec=grid_spec,
        out_shape=jax.ShapeDtypeStruct((rows * MOE_TOP_K + MOE_BLOCK, d), f32),
        compiler_params=_params(1),
        name="moe_experts",
    )(block_expert, n_used, slot_dst, h2, wg, wu, wd)


def _argmax_cols(cols):
    idx = jnp.zeros(cols[0].shape, jnp.int32)
    best = cols[0]
    for j in range(1, len(cols)):
        better = cols[j] > best
        idx = jnp.where(better, j, idx)
        best = jnp.where(better, cols[j], best)
    return idx, best


def _moe_route(logits):
    g = [logits[:, j] for j in range(MOE_GROUPS)]
    g_idx, g_max = _argmax_cols(g)
    g_prob = 1.0 / sum(jnp.exp(gj - g_max) for gj in g)
    e = []
    for j in range(MOE_EPG):
        ej = logits[:, MOE_GROUPS + j]
        for grp in range(1, MOE_GROUPS):
            ej = jnp.where(g_idx == grp, logits[:, MOE_GROUPS + grp * MOE_EPG + j], ej)
        e.append(ej)
    i1, v1 = _argmax_cols(e)
    i2, v2 = _argmax_cols([jnp.where(i1 == j, -jnp.inf, e[j]) for j in range(MOE_EPG)])
    t = jnp.exp(v2 - v1)
    weight = jnp.stack([1.0 / (1.0 + t), t / (1.0 + t)], axis=-1) * g_prob[:, None]
    expert = g_idx[:, None] * MOE_EPG + jnp.stack([i1, i2], axis=-1)
    return expert, weight


def _moe_dispatch(expert):
    expert = expert.reshape(-1)
    n_assign = expert.shape[0]
    ids = jnp.arange(MOE_EXPERTS, dtype=jnp.int32)
    onehot = (expert[:, None] == ids[None, :]).astype(jnp.int32)
    csum = jnp.cumsum(onehot, axis=0)
    rank = jnp.sum(onehot * csum, axis=1) - 1
    counts = csum[-1]
    padded = (counts + MOE_BLOCK - 1) // MOE_BLOCK * MOE_BLOCK
    pad_end = jnp.cumsum(padded)
    dest = jnp.sum(onehot * (pad_end - padded)[None, :], axis=1) + rank
    n_blocks = -(-n_assign // MOE_BLOCK) + MOE_EXPERTS
    slot_dst = jnp.full((n_blocks * MOE_BLOCK,), -1, jnp.int32).at[dest].set(
        jnp.arange(n_assign, dtype=jnp.int32))
    starts = jnp.arange(n_blocks, dtype=jnp.int32) * MOE_BLOCK
    block_expert = jnp.minimum(jnp.sum((pad_end[None, :] <= starts[:, None]).astype(jnp.int32), axis=1),
                               MOE_EXPERTS - 1)
    n_used = pad_end[-1:] // MOE_BLOCK
    return slot_dst, block_expert.astype(jnp.int32), n_used.astype(jnp.int32)


def _rope_swap_cols(w):
    qd = MLA_ROPE // 4
    return jnp.concatenate([-w[..., qd:2 * qd], w[..., :qd], -w[..., 3 * qd:], w[..., 2 * qd:3 * qd]], axis=-1)


def _rope_tables(seq, tm):
    qd = MLA_ROPE // 4
    n_rows = seq // GRID_W
    inv_freq = ROPE_THETA ** (-jnp.arange(qd, dtype=f32) / qd)
    row = jnp.repeat(jnp.arange(n_rows, dtype=f32), GRID_W)
    col = jnp.tile(jnp.arange(GRID_W, dtype=f32), n_rows)
    ar, ac = row[:, None] * inv_freq, col[:, None] * inv_freq
    zeros = jnp.zeros((seq, LANES - MLA_ROPE), f32)
    cos_t = jnp.concatenate([jnp.cos(ar), jnp.cos(ar), jnp.cos(ac), jnp.cos(ac), zeros], axis=-1)
    sin_t = jnp.concatenate([jnp.sin(ar), jnp.sin(ar), jnp.sin(ac), jnp.sin(ac), zeros], axis=-1)
    ident = jnp.concatenate([jnp.ones((tm, MLA_ROPE), f32), jnp.zeros((tm, LANES - MLA_ROPE), f32)], axis=-1)
    return (jnp.concatenate([ident, cos_t], axis=0),
            jnp.concatenate([jnp.zeros((tm, LANES), f32), sin_t], axis=0))


def _layer(x_parts, modv, lw, dims, with_ctx, final_g):
    b, s, n_ctx, d = dims["B"], dims["S"], dims["n_ctx"], dims["D"]
    nc, nl, n = dims["Nc"], dims["Nl"], dims["N"]
    tile = dims["tile"]

    def mod_blk(which, tm, row0):
        def im(i, *_):
            g0 = i * tm + row0
            cond = jnp.where(g0 < nc, b, jnp.maximum(g0 - nc, 0) // s)
            return (cond * 6 + which, 0, 0)
        return im

    def part_blocks(parts, tm, blk0):
        if len(parts) == 1:
            return 0, [lambda i: i + blk0]
        n_first = parts[0].shape[0] // tm
        return n_first, [lambda i: jnp.minimum(i, n_first - 1), lambda i: jnp.maximum(i - n_first, 0)]

    def norm_mod(parts, g, which_shift, which_scale, rows, row0, dtype):
        tm = min(256, tile)
        n_first, blks = part_blocks(parts, tm, 0)
        return pl.pallas_call(
            functools.partial(_norm_mod_body, n_first=n_first),
            grid=(rows // tm,),
            in_specs=[pl.BlockSpec((tm, d), lambda i, f=f: (f(i), 0)) for f in blks]
            + [pl.BlockSpec((1, d), lambda i: (0, 0)),
               pl.BlockSpec((1, 1, d), mod_blk(which_shift, tm, row0)),
               pl.BlockSpec((1, 1, d), mod_blk(which_scale, tm, row0))],
            out_specs=pl.BlockSpec((tm, d), lambda i: (i, 0)),
            out_shape=jax.ShapeDtypeStruct((rows, d), dtype),
            compiler_params=_params(1),
            name="norm_modulate",
        )(*parts, g.reshape(1, d), modv, modv)

    h = norm_mod(x_parts, lw["norm1_g"], 0, 1, n, 0, bf16)

    tm = min(1024, tile)
    row_tile = lambda i, j: (i, 0)

    def full_out(cols, dtype, tn):
        return (jax.ShapeDtypeStruct((n, cols), dtype), (tm, tn), lambda i, j: (i, j))

    w_in_t, in_base, in_offs = lw["w_in_t"], lw["in_base"], lw["in_offs"]
    cqn, = _mm(h, w_in_t, _ep_rmsnorm, tm=tm, tn=MLA_Q_RANK, rows=n, ncols=MLA_Q_RANK, w_is_t=True,
               w_row_of=lambda j: in_base + in_offs[0],
               extras=[(lw["q_norm_g"], (1, MLA_Q_RANK), lambda i, j: (0, 0))],
               outs=[full_out(MLA_Q_RANK, bf16, MLA_Q_RANK)], name="in_proj_cq")
    ckvn, = _mm(h, w_in_t, _ep_rmsnorm, tm=tm, tn=MLA_KV_RANK, rows=n, ncols=MLA_KV_RANK, w_is_t=True,
                w_row_of=lambda j: in_base + in_offs[1],
                extras=[(lw["kv_norm_g"], (1, MLA_KV_RANK), lambda i, j: (0, 0))],
                outs=[full_out(MLA_KV_RANK, bf16, MLA_KV_RANK)], name="in_proj_ckv")

    cos_t, sin_t = _rope_tables(s, tm)

    def rope_blk(i, j):
        g0 = i * tm
        return (jnp.where(g0 < nc, 0, 1 + (jnp.maximum(g0 - nc, 0) % s) // tm), 0)

    rope_extras = [(cos_t, (tm, LANES), rope_blk), (sin_t, (tm, LANES), rope_blk)]
    kr, ga = _mm(h, lw["w_small"], _ep_rope_small, tm=tm, tn=2 * LANES, rows=n, ncols=2 * LANES, w_is_t=True,
                 extras=rope_extras,
                 outs=[(jax.ShapeDtypeStruct((n, LANES), bf16), (tm, LANES), row_tile),
                       (jax.ShapeDtypeStruct((n, LANES), f32), (tm, LANES), row_tile)],
                 name="in_proj_small")
    big_runs = [(in_offs[3], in_offs[7]), (in_offs[9], in_offs[11]), (in_offs[13], in_offs[14]),
                (in_offs[14], in_offs[17])]
    n_big = sum(e - a for a, e in big_runs)
    tn_big = min(1024, d)
    assert all((e - a) % tn_big == 0 for a, e in big_runs)

    def big_row(j):
        row, t0 = None, 0
        for a, e in big_runs:
            r = in_base + a + (j - t0) * tn_big
            row = r if row is None else jnp.where(j >= t0, r, row)
            t0 += (e - a) // tn_big
        return row

    big, = _mm(h, w_in_t, _ep_cast, tm=tm, tn=tn_big, rows=n, ncols=n_big, w_is_t=True, w_row_of=big_row,
               outs=[full_out(n_big, bf16, tn_big)], name="in_proj_big")
    n_hf = in_offs[13] - in_offs[11]
    hf, = _mm(h, w_in_t, _ep_cast, tm=tm, tn=1024, rows=n, ncols=n_hf, w_is_t=True,
              w_row_of=lambda j: in_base + in_offs[11] + j * 1024,
              outs=[full_out(n_hf, f32, 1024)], name="in_proj_hf")

    qw = MLA_HEADS * MLA_QK
    q, = _mm(cqn, lw["w_uq"], _ep_rope_q, tm=tm, tn=1024, rows=n, ncols=qw,
             extras=rope_extras, outs=[full_out(qw, bf16, 1024)], name="mla_q_up")
    kv, = _mm(ckvn, lw["w_ukv"], _ep_cast, tm=tm, tn=1024, rows=n, ncols=qw,
              outs=[full_out(qw, bf16, 1024)], name="mla_kv_up")
    o_a_ctx, o_a_lat = _attention(q, kv, kr, dims, with_ctx)

    gk0 = GLA_HEADS * GLA_DK
    gv0 = 2 * gk0
    gr0 = gv0 + GLA_HEADS * GLA_DV
    hq0 = gr0 + GLA_HEADS * GLA_DV
    hi0 = hq0 + HGRN_HEADS * HGRN_DK
    hg0 = hi0 + HGRN_HEADS * HGRN_DV
    z0 = hg0 + HGRN_HEADS * HGRN_DV
    o_b_ctx, o_b_lat = _scan_call(
        "gla",
        [(big, GLA_DK, lambda hh: hh), (big, GLA_DK, lambda hh: gk0 // GLA_DK + hh),
         (big, GLA_DV, lambda hh: gv0 // GLA_DV + hh), (big, GLA_DV, lambda hh: gr0 // GLA_DV + hh),
         (ga, LANES, lambda hh: 0)],
        [(lw["gla_wa"], (1, 2, LANES, GLA_DK), lambda i, hh: (hh, 0, 0, 0)),
         (lw["gla_ba"], (1, 2, 1, GLA_DK), lambda i, hh: (hh, 0, 0, 0)),
         (lw["gla_onorm_g"], (1, GLA_DV), lambda i, hh: (0, 0))],
        dims, GLA_HEADS, GLA_DV)

    o_c_ctx, o_c_lat = _scan_call(
        "hgrn",
        [(big, HGRN_DK, lambda hh: hq0 // HGRN_DK + hh), (big, HGRN_DV, lambda hh: hi0 // HGRN_DV + hh),
         (big, HGRN_DV, lambda hh: hg0 // HGRN_DV + hh),
         (hf, HGRN_DK, lambda hh: hh), (hf, HGRN_DK, lambda hh: HGRN_HEADS + hh)],
        [(lw["hgrn_log_lb"], (1, 1, HGRN_DK), lambda i, hh: (hh, 0, 0)),
         (lw["hgrn_log_1mlb"], (1, 1, HGRN_DK), lambda i, hh: (hh, 0, 0)),
         (lw["hgrn_1mlb"], (1, 1, HGRN_DK), lambda i, hh: (hh, 0, 0)),
         (lw["hgrn_onorm_g"], (1, HGRN_DV), lambda i, hh: (0, 0))],
        dims, HGRN_HEADS, HGRN_DV)

    if with_ctx:
        o_a = jnp.concatenate([o_a_ctx, o_a_lat], axis=0)
        o_b = jnp.concatenate([o_b_ctx, o_b_lat], axis=0)
        o_c = jnp.concatenate([o_c_ctx, o_c_lat], axis=0)
        rows, row0 = n, 0
    else:
        o_a, o_b, o_c = o_a_lat, o_b_lat, o_c_lat
        rows, row0 = nl, nc
    tmo = min(1024, tile)
    layer = lw["layer"]
    m = _merge(o_a, o_b, o_c, lw["w_branch"], layer, big, z0, rows, row0, tmo, d)
    tno = min(1024, d)
    tmp = tmo // len(x_parts)
    n_first, blks = part_blocks(x_parts, tmp, row0 // tmp)
    x1, = _mm(m, lw["w_out"], functools.partial(_ep_residual, n_first=n_first), tm=tmp, tn=tno, rows=rows,
              ncols=d, w_blk0=layer,
              extras=[(part, (tmp, tno), lambda i, j, f=f: (f(i), j)) for part, f in zip(x_parts, blks)]
              + [(modv, (1, 1, tno), lambda i, j: mod_blk(2, tmp, row0)(i)[:1] + (0, j))],
              outs=[(jax.ShapeDtypeStruct((rows, d), f32), (tmp, tno), lambda i, j: (i, j))],
              name="out_proj_residual")

    h2 = norm_mod((x1,), lw["norm2_g"], 3, 4, rows, row0, f32)
    tmr = min(512, tile)
    logits, = _mm(h2, lw["w_router"], _ep_bias, tm=tmr, tn=LANES, rows=rows, ncols=LANES,
                  extras=[(lw["b_router"], (1, LANES), lambda i, j: (0, 0))],
                  outs=[(jax.ShapeDtypeStruct((rows, LANES), f32), (tmr, LANES), row_tile)],
                  name="moe_router")
    expert, weight = _moe_route(logits)
    slot_dst, block_expert, n_used = _moe_dispatch(expert)
    y2 = _moe_experts(h2, slot_dst, block_expert + layer * MOE_EXPERTS, n_used,
                      lw["moe_w_gate"], lw["moe_w_up"], lw["moe_w_down"])

    tmc = min(256, tile)
    in_specs = [pl.BlockSpec((tmc, d), lambda i: (i, 0)),
                pl.BlockSpec((tmc, d), lambda i: (i, 0)),
                pl.BlockSpec((tmc, d), lambda i: (rows // tmc + i, 0)),
                pl.BlockSpec((tmc, MOE_TOP_K), lambda i: (i, 0)),
                pl.BlockSpec((1, 1, d), mod_blk(5, tmc, row0))]
    args = [x1, y2, y2, weight, modv]
    if final_g is None:
        body, name = _combine_body, "moe_combine"
    else:
        body, name = _final_norm_body, "moe_combine_final_norm"
        in_specs.append(pl.BlockSpec((1, d), lambda i: (0, 0)))
        args.append(final_g.reshape(1, d))
    return pl.pallas_call(
        body,
        grid=(rows // tmc,),
        in_specs=in_specs,
        out_specs=pl.BlockSpec((tmc, d), lambda i: (i, 0)),
        out_shape=jax.ShapeDtypeStruct((rows, d), f32),
        compiler_params=_params(1),
        name=name,
    )(*args)


def _layer_weights(layer, d, w_in, w_in_t, mla_q_norm_g, mla_kv_norm_g, mla_w_uq, mla_w_ukv, gla_w_a2, gla_b_a,
                   gla_onorm_g, hgrn_lb, hgrn_onorm_g, w_branch, w_out, norm1_g, norm2_g,
                   moe_w_group, moe_b_group, moe_w_expert, moe_b_expert, moe_w_gate, moe_w_up, moe_w_down):
    widths = (MLA_Q_RANK, MLA_KV_RANK, MLA_ROPE,
              GLA_HEADS * GLA_DK, GLA_HEADS * GLA_DK, GLA_HEADS * GLA_DV, GLA_HEADS * GLA_DV,
              GLA_GATE_RANK, GLA_GATE_RANK,
              HGRN_HEADS * HGRN_DK, HGRN_HEADS * HGRN_DV, HGRN_HEADS * HGRN_DK, HGRN_HEADS * HGRN_DK,
              HGRN_HEADS * HGRN_DV, d, d, d)
    offs = [0]
    for w in widths:
        offs.append(offs[-1] + w)
    assert offs[-1] == w_in.shape[2]
    def col_t(a, e):
        piece = lax.slice(w_in, (layer, 0, offs[a]), (layer + 1, d, offs[e]))
        return jnp.swapaxes(piece, 1, 2).reshape(offs[e] - offs[a], d)

    w_kr = col_t(2, 3)
    w_small = jnp.concatenate(
        [w_kr, _rope_swap_cols(w_kr.T).T, col_t(7, 9),
         jnp.zeros((2 * LANES - 2 * MLA_ROPE - 2 * GLA_GATE_RANK, d), f32)], axis=0)
    uq = mla_w_uq[layer]
    uq_rope = uq[..., MLA_NOPE:]
    w_uq = jnp.concatenate([uq[..., :MLA_NOPE], uq_rope, _rope_swap_cols(uq_rope)], axis=-1)
    wa = jnp.zeros((2, LANES, GLA_HEADS * GLA_DK), f32)
    wa = wa.at[0, :GLA_GATE_RANK].set(gla_w_a2[layer, 0])
    wa = wa.at[1, GLA_GATE_RANK:2 * GLA_GATE_RANK].set(gla_w_a2[layer, 1])
    wa = wa.reshape(2, LANES, GLA_HEADS, GLA_DK).transpose(2, 0, 1, 3)
    ba = gla_b_a[layer].reshape(2, GLA_HEADS, 1, GLA_DK).transpose(1, 0, 2, 3)
    lb = hgrn_lb[layer].reshape(HGRN_HEADS, 1, HGRN_DK)
    w_router = jnp.concatenate(
        [moe_w_group[layer], moe_w_expert[layer], jnp.zeros((d, LANES - MOE_GROUPS - MOE_EXPERTS), f32)], axis=-1)
    b_router = jnp.concatenate(
        [moe_b_group[layer], moe_b_expert[layer], jnp.zeros((LANES - MOE_GROUPS - MOE_EXPERTS,), f32)])
    return {
        "norm1_g": norm1_g[layer], "norm2_g": norm2_g[layer],
        "w_in_t": w_in_t, "in_base": layer * offs[-1], "in_offs": offs, "w_small": w_small.astype(bf16),
        "q_norm_g": mla_q_norm_g[layer].reshape(1, -1), "kv_norm_g": mla_kv_norm_g[layer].reshape(1, -1),
        "w_uq": w_uq.reshape(MLA_Q_RANK, MLA_HEADS * MLA_QK).astype(bf16),
        "w_ukv": mla_w_ukv[layer].reshape(MLA_KV_RANK, MLA_HEADS * (MLA_NOPE + MLA_V)).astype(bf16),
        "gla_wa": wa.astype(bf16), "gla_ba": ba, "gla_onorm_g": gla_onorm_g[layer].reshape(1, -1),
        "hgrn_log_lb": jnp.log(lb) * LOG2E, "hgrn_log_1mlb": jnp.log1p(-lb) * LOG2E, "hgrn_1mlb": 1.0 - lb,
        "hgrn_onorm_g": hgrn_onorm_g[layer].reshape(1, -1),
        "w_router": w_router.astype(bf16), "b_router": b_router.reshape(1, LANES),
        "layer": layer, "w_branch": w_branch, "w_out": w_out,
        "moe_w_gate": moe_w_gate, "moe_w_up": moe_w_up, "moe_w_down": moe_w_down,
    }


def kernel(x, c, ctx, c_ctx, w_mod, b_mod, norm1_g, w_in, mla_q_norm_g, mla_kv_norm_g, mla_w_uq, mla_w_ukv, gla_w_a2, gla_b_a, gla_onorm_g, hgrn_lb_logits, hgrn_onorm_g, w_branch, w_out, norm2_g, moe_w_group, moe_b_group, moe_w_expert, moe_b_expert, moe_w_gate, moe_w_up, moe_w_down, final_norm_g):
    b, s, d = x.shape
    n_ctx = ctx.shape[1]
    depth = w_mod.shape[0]
    nc, nl = b * n_ctx, b * s
    assert nc % s == 0 and s % GRID_W == 0 and n_ctx % SCAN_CHUNK == 0 and s % SCAN_CHUNK == 0
    tile = 1
    while nc % (tile * 2) == 0 and s % (tile * 2) == 0:
        tile *= 2
    dims = {"B": b, "S": s, "n_ctx": n_ctx, "D": d, "Nc": nc, "Nl": nl, "N": nc + nl, "tile": tile}

    n_cond = -(-(b + 1) // 8) * 8
    cond = jnp.concatenate([c, c_ctx[None, :], jnp.zeros((n_cond - b - 1, d), f32)], axis=0)
    mod_all = _mod_vectors(cond, w_mod, b_mod)

    lb_cum = jnp.cumsum(jax.nn.softmax(hgrn_lb_logits.astype(f32), axis=0), axis=0)
    hgrn_lb = lb_cum - lb_cum[:1]

    hid = moe_w_gate.shape[-1]
    w_in_t = jnp.swapaxes(w_in, 1, 2).astype(bf16).reshape(depth * w_in.shape[2], d)
    wb_all = w_branch.astype(bf16).reshape(depth * w_branch.shape[1], BRANCH_WIDTH, d)
    wo_all = w_out.astype(bf16).reshape(depth * d, d)
    wg_all = moe_w_gate.astype(bf16).reshape(depth * MOE_EXPERTS, d, hid)
    wu_all = moe_w_up.astype(bf16).reshape(depth * MOE_EXPERTS, d, hid)
    wd_all = moe_w_down.astype(bf16).reshape(depth * MOE_EXPERTS, hid, d)

    x_parts = (ctx.reshape(nc, d), x.reshape(nl, d))
    for layer in range(depth):
        with_ctx = layer < depth - 1
        lw = _layer_weights(layer, d, w_in, w_in_t, mla_q_norm_g, mla_kv_norm_g, mla_w_uq, mla_w_ukv, gla_w_a2,
                            gla_b_a, gla_onorm_g, hgrn_lb, hgrn_onorm_g, wb_all, wo_all, norm1_g, norm2_g,
                            moe_w_group, moe_b_group, moe_w_expert, moe_b_expert, wg_all, wu_all, wd_all)
        modv = mod_all[layer, :b + 1].reshape((b + 1) * 6, 1, d)
        x_parts = (_layer(x_parts, modv, lw, dims, with_ctx, None if with_ctx else final_norm_g),)
    return x_parts[0].reshape(b, s, d)
```
